```python
import math
import jax, jax.numpy as jnp
from jax import lax
import numpy as np

D_MODEL = 1024
BATCH = 2
SEQ = 8192
DEPTH = 4
DEC_BATCH = 128
DEC_SEQ = 8
PAST_LEN = 2048
PAGE_SIZE = 128

RW_HEAD = 64
RW_WIDTH = D_MODEL // 2
RW_HEADS = RW_WIDTH // RW_HEAD
W_LORA = 64
A_LORA = 64
G_LORA = 128
SHIFT_W = 3 * RW_WIDTH + W_LORA + A_LORA + G_LORA
DA_SUB = 64
DA_WIDTH = D_MODEL // 2
DA_HEADS = DA_WIDTH // (2 * DA_SUB)
ROT_DIM = DA_SUB // 4
ROPE_THETA = 500000.0
IN_W = SHIFT_W + 3 * DA_WIDTH + 2 * D_MODEL
D_FF = 2816
CONV_W = 3
Q_BLOCK = 128
NORM_EPS = 1e-6
LNX_EPS = 64e-5
SUBLN_EPS = 1e-5

kernel_name = 'rwkv7_diffattn_gated_hybrid_step'


def rms_norm(x, g, eps=NORM_EPS):
    x32 = x.astype(jnp.float32)
    y = x32 * lax.rsqrt(jnp.mean(x32 * x32, axis=-1, keepdims=True) + eps)
    return (y * g.astype(jnp.float32)).astype(x.dtype)


def rope_partial(x, pos):
    half = ROT_DIM // 2
    inv = jnp.exp(-math.log(ROPE_THETA) * jnp.arange(half, dtype=jnp.float32) * (2.0 / ROT_DIM))
    ang = pos.astype(jnp.float32)[:, None] * inv[None, :]
    cos = jnp.cos(ang)[None, :, None, None, :]
    sin = jnp.sin(ang)[None, :, None, None, :]
    xr = x[..., :ROT_DIM].astype(jnp.float32)
    x1, x2 = xr[..., :half], xr[..., half:]
    rot = jnp.concatenate([x1 * cos - x2 * sin, x2 * cos + x1 * sin], axis=-1).astype(x.dtype)
    return jnp.concatenate([rot, x[..., ROT_DIM:]], axis=-1)


def diff_attn_block(q, k, v, q_pos, k_pos, lam):
    s = jnp.einsum('bqhcd,bkhcd->bhcqk', q, k).astype(jnp.float32) * (DA_SUB ** -0.5)
    causal = k_pos[None, :] <= q_pos[:, None]
    s = jnp.where(causal, s, -jnp.inf)
    p = jax.nn.softmax(s, axis=-1)
    wts = p[:, :, 0] - lam * p[:, :, 1]
    return jnp.einsum('bhqk,bkhe->bqhe', wts.astype(v.dtype), v)


def diff_attention(q, k, v, q_pos, k_pos, lam):
    b, tq = q.shape[0], q.shape[1]
    blk = math.gcd(Q_BLOCK, tq)
    nb = tq // blk
    qb = jnp.moveaxis(q.reshape((b, nb, blk) + q.shape[2:]), 1, 0)
    pb = q_pos.reshape(nb, blk)
    out = lax.map(lambda qp: diff_attn_block(qp[0], k, v, qp[1], k_pos, lam), (qb, pb))
    return jnp.moveaxis(out, 0, 1).reshape((b, tq) + v.shape[2:])


def wkv7_scan(r, w, k, v, kk, a, s0):
    xs = tuple(jnp.moveaxis(t.astype(jnp.float32), 1, 0) for t in (r, w, k, v, kk, a))

    def step(s, inp):
        r_t, w_t, k_t, v_t, kk_t, a_t = inp
        sa = jnp.einsum('bhij,bhj->bhi', s, -kk_t)
        s = (s * w_t[:, :, None, :] + sa[..., None] * (kk_t * a_t)[:, :, None, :]
             + v_t[..., None] * k_t[:, :, None, :])
        y = jnp.einsum('bhij,bhj->bhi', s, r_t)
        return s, y

    s, ys = lax.scan(step, s0.astype(jnp.float32), xs)
    return jnp.moveaxis(ys, 0, 1), s


def rwkv7_branch(rs, s0, lp):
    b, t, _ = rs.shape
    idx = [RW_WIDTH, 2 * RW_WIDTH, 3 * RW_WIDTH, 3 * RW_WIDTH + W_LORA, 3 * RW_WIDTH + W_LORA + A_LORA]
    r, k, v, wl, al, gl = jnp.split(rs, idx, axis=-1)

    def hd(z):
        return z.reshape(b, t, RW_HEADS, RW_HEAD)

    w_log = -jax.nn.softplus(-(lp['rw_w0'] + jnp.tanh(wl) @ lp['rw_w_lora'])) - 0.5
    decay = jnp.exp(-jnp.exp(w_log.astype(jnp.float32)))
    a = jax.nn.sigmoid(lp['rw_a0'] + al @ lp['rw_a_lora'])
    g = jax.nn.sigmoid(gl) @ lp['rw_g_lora']
    kk = hd(k * lp['rw_k_k']).astype(jnp.float32)
    kk = kk * lax.rsqrt(jnp.sum(kk * kk, axis=-1, keepdims=True) + 1e-12)
    k = k * (1.0 + (a - 1.0) * lp['rw_k_a'])
    rh, kh, vh = hd(r), hd(k), hd(v)
    y, s = wkv7_scan(rh, hd(decay), kh, vh, kk, hd(a), s0)
    mu = jnp.mean(y, axis=-1, keepdims=True)
    var = jnp.mean(jnp.square(y - mu), axis=-1, keepdims=True)
    yn = ((y - mu) * lax.rsqrt(var + LNX_EPS)).reshape(b, t, RW_WIDTH)
    yn = yn * lp['rw_lnx_w'].astype(jnp.float32) + lp['rw_lnx_b'].astype(jnp.float32)
    bonus = jnp.sum((rh * kh * lp['rw_r_k']).astype(jnp.float32), axis=-1, keepdims=True) * vh.astype(jnp.float32)
    out = (yn + bonus.reshape(b, t, RW_WIDTH)) * g.astype(jnp.float32)
    return out.astype(rs.dtype), s


def token_mixer(h, shift_prev, wkv_prev, k_past, v_past, layer_idx, lp):
    b, t, _ = h.shape
    past = k_past.shape[1]
    pos = past + jnp.arange(t, dtype=jnp.int32)
    k_pos = jnp.arange(past + t, dtype=jnp.int32)
    proj = h @ lp['w_in']
    o1 = SHIFT_W
    o2 = o1 + DA_WIDTH
    o3 = o2 + DA_WIDTH
    o4 = o3 + DA_WIDTH
    o5 = o4 + D_MODEL
    rw, dq, dk, dv, g_a, g_b = jnp.split(proj, [o1, o2, o3, o4, o5], axis=-1)
    prev = jnp.concatenate([shift_prev[:, None, :].astype(rw.dtype), rw[:, :-1]], axis=1)
    rs = rw + lp['rw_mu'] * (prev - rw)
    o_a, wkv_new = rwkv7_branch(rs, wkv_prev, lp)
    q = rope_partial(dq.reshape(b, t, DA_HEADS, 2, DA_SUB), pos)
    k = rope_partial(dk.reshape(b, t, DA_HEADS, 2, DA_SUB), pos)
    v = dv.reshape(b, t, DA_HEADS, 2 * DA_SUB)
    k_all = jnp.concatenate([k_past.astype(k.dtype), k], axis=1)
    v_all = jnp.concatenate([v_past.astype(v.dtype), v], axis=1)
    lam_init = 0.8 - 0.6 * math.exp(-0.3 * layer_idx)
    f32 = jnp.float32
    lam = (jnp.exp(jnp.sum(lp['da_lam_q1'].astype(f32) * lp['da_lam_k1'].astype(f32)))
           - jnp.exp(jnp.sum(lp['da_lam_q2'].astype(f32) * lp['da_lam_k2'].astype(f32))) + lam_init)
    o = diff_attention(q, k_all, v_all, pos, k_pos, lam)
    o_b = (rms_norm(o, lp['da_subln'], SUBLN_EPS) * (1.0 - lam_init)).reshape(b, t, DA_WIDTH)
    m = jax.nn.sigmoid(g_a) * (o_a @ lp['w_pa']) + jax.nn.sigmoid(g_b) * (o_b @ lp['w_pb'])
    return m @ lp['w_o'], k, v, wkv_new.astype(h.dtype), rw[:, -1]


def conv_ffn(h, conv_prev, lp):
    t = h.shape[1]
    up = h @ lp['w_up']
    u, z = up[..., :D_FF], up[..., D_FF:]
    u_pad = jnp.concatenate([conv_prev.astype(u.dtype), u], axis=1)
    uc = lp['conv_b']
    for j in range(CONV_W):
        uc = uc + u_pad[:, j:j + t] * lp['conv_w'][j]
    f = (jax.nn.gelu(uc, approximate=False) * z) @ lp['w_down']
    return f, u_pad[:, -(CONV_W - 1):]


def trunk_layer(x, shift_prev, wkv_prev, conv_prev, k_past, v_past, layer_idx, lp):
    a, k_new, v_new, wkv_new, shift_new = token_mixer(
        rms_norm(x, lp['norm_mix_pre']), shift_prev, wkv_prev, k_past, v_past, layer_idx, lp)
    x = x + rms_norm(a, lp['norm_mix_post'])
    f, conv_new = conv_ffn(rms_norm(x, lp['norm_ffn_pre']), conv_prev, lp)
    x = x + rms_norm(f, lp['norm_ffn_post'])
    return x, k_new, v_new, wkv_new, shift_new, conv_new


def setup_inputs(seed: int = 0) -> dict:
    key = jax.random.key(seed)
    ks = iter(jax.random.split(key, 40))
    f32 = jnp.float32

    def nrm(shape, scale):
        return jax.random.normal(next(ks), shape, f32) * scale

    def gain(shape):
        return 1.0 + nrm(shape, 0.05)

    n_pages = PAST_LEN // PAGE_SIZE
    n_used = DEC_BATCH * n_pages
    n_pool = n_used + max(1, n_used // 4)
    L = DEPTH
    x_prompt = nrm((BATCH, SEQ, D_MODEL), 1.0)
    x_sample = nrm((DEC_BATCH, DEC_SEQ, D_MODEL), 1.0)
    cache_k = nrm((L, n_pool, PAGE_SIZE, DA_HEADS, 2, DA_SUB), 1.0)
    cache_v = nrm((L, n_pool, PAGE_SIZE, DA_HEADS, 2 * DA_SUB), 1.0)
    state_wkv = nrm((L, DEC_BATCH, RW_HEADS, RW_HEAD, RW_HEAD), 0.5)
    state_shift = nrm((L, DEC_BATCH, SHIFT_W), 1.0)
    state_conv = nrm((L, DEC_BATCH, CONV_W - 1, D_FF), 1.0)
    page_table = jax.random.permutation(next(ks), n_pool)[:n_used].reshape(DEC_BATCH, n_pages).astype(jnp.int32)
    return {
        'x_prompt': x_prompt,
        'x_sample': x_sample,
        'cache_k': cache_k,
        'cache_v': cache_v,
        'state_wkv': state_wkv,
        'state_shift': state_shift,
        'state_conv': state_conv,
        'page_table': page_table,
        'norm_mix_pre': gain((L, D_MODEL)),
        'norm_mix_post': gain((L, D_MODEL)),
        'norm_ffn_pre': gain((L, D_MODEL)),
        'norm_ffn_post': gain((L, D_MODEL)),
        'w_in': nrm((L, D_MODEL, IN_W), D_MODEL ** -0.5),
        'rw_mu': jax.random.uniform(next(ks), (L, SHIFT_W), f32),
        'rw_w0': nrm((L, RW_WIDTH), 0.5),
        'rw_w_lora': nrm((L, W_LORA, RW_WIDTH), 0.5 * W_LORA ** -0.5),
        'rw_a0': nrm((L, RW_WIDTH), 0.5),
        'rw_a_lora': nrm((L, A_LORA, RW_WIDTH), A_LORA ** -0.5),
        'rw_g_lora': nrm((L, G_LORA, RW_WIDTH), G_LORA ** -0.5),
        'rw_k_k': 0.85 + nrm((L, RW_WIDTH), 0.1),
        'rw_k_a': 1.0 + nrm((L, RW_WIDTH), 0.1),
        'rw_r_k': nrm((L, RW_HEADS, RW_HEAD), 0.1),
        'rw_lnx_w': gain((L, RW_WIDTH)),
        'rw_lnx_b': nrm((L, RW_WIDTH), 0.01),
        'da_lam_q1': nrm((L, DA_SUB), 0.1),
        'da_lam_k1': nrm((L, DA_SUB), 0.1),
        'da_lam_q2': nrm((L, DA_SUB), 0.1),
        'da_lam_k2': nrm((L, DA_SUB), 0.1),
        'da_subln': gain((L, 2 * DA_SUB)),
        'w_pa': nrm((L, RW_WIDTH, D_MODEL), RW_WIDTH ** -0.5),
        'w_pb': nrm((L, DA_WIDTH, D_MODEL), DA_WIDTH ** -0.5),
        'w_o': nrm((L, D_MODEL, D_MODEL), D_MODEL ** -0.5),
        'w_up': nrm((L, D_MODEL, 2 * D_FF), D_MODEL ** -0.5),
        'conv_w': nrm((L, CONV_W, D_FF), CONV_W ** -0.5),
        'conv_b': nrm((L, D_FF), 0.01),
        'w_down': nrm((L, D_FF, D_MODEL), D_FF ** -0.5),
    }


def reference(x_prompt, x_sample, cache_k, cache_v, state_wkv, state_shift, state_conv, page_table,
              norm_mix_pre, norm_mix_post, norm_ffn_pre, norm_ffn_post, w_in, rw_mu, rw_w0, rw_w_lora,
              rw_a0, rw_a_lora, rw_g_lora, rw_k_k, rw_k_a, rw_r_k, rw_lnx_w, rw_lnx_b,
              da_lam_q1, da_lam_k1, da_lam_q2, da_lam_k2, da_subln, w_pa, w_pb, w_o,
              w_up, conv_w, conv_b, w_down):
    bp = x_prompt.shape[0]
    db = x_sample.shape[0]
    past_len = page_table.shape[1] * cache_k.shape[2]
    dt = x_prompt.dtype
    zero_k = jnp.zeros((bp, 0) + cache_k.shape[3:], dt)
    zero_v = jnp.zeros((bp, 0) + cache_v.shape[3:], dt)
    zero_shift = jnp.zeros((bp, SHIFT_W), dt)
    zero_wkv = jnp.zeros((bp, RW_HEADS, RW_HEAD, RW_HEAD), dt)
    zero_conv = jnp.zeros((bp, CONV_W - 1, D_FF), dt)
    yp, ys = x_prompt, x_sample
    kp_l, vp_l, wp_l, sp_l, cp_l = [], [], [], [], []
    ks_l, vs_l, ws_l, ss_l, cs_l = [], [], [], [], []
    for l in range(DEPTH):
        lp = {
            'norm_mix_pre': norm_mix_pre[l], 'norm_mix_post': norm_mix_post[l],
            'norm_ffn_pre': norm_ffn_pre[l], 'norm_ffn_post': norm_ffn_post[l],
            'w_in': w_in[l], 'rw_mu': rw_mu[l], 'rw_w0': rw_w0[l], 'rw_w_lora': rw_w_lora[l],
            'rw_a0': rw_a0[l], 'rw_a_lora': rw_a_lora[l], 'rw_g_lora': rw_g_lora[l],
            'rw_k_k': rw_k_k[l], 'rw_k_a': rw_k_a[l], 'rw_r_k': rw_r_k[l],
            'rw_lnx_w': rw_lnx_w[l], 'rw_lnx_b': rw_lnx_b[l],
            'da_lam_q1': da_lam_q1[l], 'da_lam_k1': da_lam_k1[l],
            'da_lam_q2': da_lam_q2[l], 'da_lam_k2': da_lam_k2[l], 'da_subln': da_subln[l],
            'w_pa': w_pa[l], 'w_pb': w_pb[l], 'w_o': w_o[l],
            'w_up': w_up[l], 'conv_w': conv_w[l], 'conv_b': conv_b[l], 'w_down': w_down[l],
        }
        yp, k_n, v_n, wkv_n, sh_n, cv_n = trunk_layer(yp, zero_shift, zero_wkv, zero_conv, zero_k, zero_v, l, lp)
        kp_l.append(k_n); vp_l.append(v_n); wp_l.append(wkv_n); sp_l.append(sh_n); cp_l.append(cv_n)
        k_past = cache_k[l, page_table].reshape((db, past_len) + cache_k.shape[3:])
        v_past = cache_v[l, page_table].reshape((db, past_len) + cache_v.shape[3:])
        ys, k_n, v_n, wkv_n, sh_n, cv_n = trunk_layer(ys, state_shift[l], state_wkv[l], state_conv[l],
                                                      k_past, v_past, l, lp)
        ks_l.append(k_n); vs_l.append(v_n); ws_l.append(wkv_n); ss_l.append(sh_n); cs_l.append(cv_n)
    return (yp, ys,
            jnp.stack(kp_l), jnp.stack(vp_l), jnp.stack(wp_l), jnp.stack(sp_l), jnp.stack(cp_l),
            jnp.stack(ks_l), jnp.stack(vs_l), jnp.stack(ws_l), jnp.stack(ss_l), jnp.stack(cs_l))
```

```python
import functools
import math

import jax
import jax.numpy as jnp
from jax import lax
from jax.experimental import pallas as pl
from jax.experimental.pallas import tpu as pltpu

F32 = jnp.float32
BF16 = jnp.bfloat16

RW_HEAD = 64
DA_SUB = 64
ROT_DIM = 16
ROPE_THETA = 500000.0
NORM_EPS = 1e-6
LNX_EPS = 64e-5
SUBLN_EPS = 1e-5
KK_EPS = 1e-12

LANES = 128
SUBLANES = 8
VMEM_LIMIT_BYTES = 48 * 1024 * 1024

PAIR = 2 * RW_HEAD
HALF_ROT = ROT_DIM // 2
TN_DIMS = (((0,), (0,)), ((), ()))
NT_DIMS = (((1,), (1,)), ((), ()))


def _params(*semantics):
    return pltpu.CompilerParams(dimension_semantics=semantics, vmem_limit_bytes=VMEM_LIMIT_BYTES)


def _sigmoid(x):
    return 1.0 / (1.0 + jnp.exp(-x))


def _rms(x, eps):
    return x * lax.rsqrt(jnp.mean(x * x, axis=-1, keepdims=True) + eps)


def _dot(a, b):
    return jnp.dot(a.astype(BF16), b.astype(BF16), preferred_element_type=F32)


def _dot_nt(a, b):
    return lax.dot_general(a.astype(BF16), b.astype(BF16), NT_DIMS, preferred_element_type=F32)


def _dot_tn(a, b):
    return lax.dot_general(a.astype(BF16), b.astype(BF16), TN_DIMS, preferred_element_type=F32)


def _split(x, parts):
    pieces = []
    rest = x
    for _ in range(parts):
        piece = rest.astype(BF16)
        pieces.append(piece)
        rest = rest - piece.astype(F32)
    return pieces


def _sel_dot(sel, x, parts):
    out = None
    for piece in _split(x, parts):
        term = jnp.dot(sel, piece, preferred_element_type=F32)
        out = term if out is None else out + term
    return out


def _dot_sel(x, sel, parts):
    out = None
    for piece in _split(x, parts):
        term = jnp.dot(piece, sel, preferred_element_type=F32)
        out = term if out is None else out + term
    return out


def _norm_matmul_kernel(*refs, epilogue, n_q_tiles):
    if epilogue == "rope":
        x_ref, g_ref, w_ref, cos_ref, sup_ref, sdn_ref, o_ref, h_ref = refs
    else:
        x_ref, g_ref, w_ref, o_ref, h_ref = refs
    j = pl.program_id(1)

    @pl.when(j == 0)
    def _():
        h_ref[...] = (_rms(x_ref[...], NORM_EPS) * g_ref[...]).astype(BF16)

    acc = jnp.dot(h_ref[...], w_ref[...], preferred_element_type=F32)
    if epilogue == "rope":
        tn = acc.shape[1]
        rot = (acc * cos_ref[...] + pltpu.roll(acc, HALF_ROT, 1) * sdn_ref[...]
               + pltpu.roll(acc, tn - HALF_ROT, 1) * sup_ref[...])
        o_ref[...] = rot * jnp.where(j < n_q_tiles, DA_SUB ** -0.5, 1.0)
    elif epilogue == "sigmoid":
        o_ref[...] = _sigmoid(acc)
    else:
        o_ref[...] = acc


def _norm_matmul(x, g, w, *, tm, tn, epilogue="none", rope=None, n_q_tiles=0):
    n, d = x.shape
    m = w.shape[1]
    assert n % tm == 0 and m % tn == 0
    in_specs = [
        pl.BlockSpec((tm, d), lambda i, j: (i, 0)),
        pl.BlockSpec((1, d), lambda i, j: (0, 0)),
        pl.BlockSpec((d, tn), lambda i, j: (0, j)),
    ]
    args = [x, g.reshape(1, d), w]
    if epilogue == "rope":
        tab_blocks = rope[0].shape[0] // tm
        assert rope[0].shape == (tab_blocks * tm, tn)
        in_specs += [pl.BlockSpec((tm, tn), lambda i, j: (i % tab_blocks, 0))] * 3
        args += list(rope)
    return pl.pallas_call(
        functools.partial(_norm_matmul_kernel, epilogue=epilogue, n_q_tiles=n_q_tiles),
        grid=(n // tm, m // tn),
        in_specs=in_specs,
        out_specs=pl.BlockSpec((tm, tn), lambda i, j: (i, j)),
        out_shape=jax.ShapeDtypeStruct((n, m), F32),
        scratch_shapes=[pltpu.VMEM((tm, d), BF16)],
        compiler_params=_params("parallel", "arbitrary"),
        name="norm_matmul_" + epilogue,
    )(*args)


def _rwkv_kernel(rw_ref, shift_ref, wkv_ref, mu_ref, w0_ref, wlora_ref, a0_ref, alora_ref, glora_ref,
                 kk_ref, ka_ref, rk_ref, lnw_ref, lnb_ref, tri_ref, ones_ref, dup_ref, coll_ref,
                 o_ref, wkv_out_ref, s_ref, prev_ref, y_ref, *, lora_in):
    i = pl.program_id(1)
    nb, c, sw = rw_ref.shape
    w = w0_ref.shape[1]
    n_pairs = w // PAIR
    c2 = 2 * c
    chains = [(s, p) for s in range(nb) for p in range(n_pairs)]
    n = len(chains)

    @pl.when(i == 0)
    def _():
        prev_ref[...] = shift_ref[...]
        blk = (lax.broadcasted_iota(jnp.int32, (PAIR, PAIR), 0) // RW_HEAD
               == lax.broadcasted_iota(jnp.int32, (PAIR, PAIR), 1) // RW_HEAD)
        for s, p in chains:
            s_ref[s, p] = jnp.where(blk, _dot_sel(wkv_ref[s, p], dup_ref[...], 3), 0.0)

    rw = rw_ref[...].reshape(nb * c, sw)
    row = lax.broadcasted_iota(jnp.int32, (nb * c, 1), 0)
    prev = pltpu.roll(rw, 1, 0)
    for s in range(nb):
        prev = jnp.where(row == s * c, prev_ref[s], prev)
        prev_ref[s] = rw[(s + 1) * c - 1:(s + 1) * c, :]
    rs = rw + mu_ref[...] * (prev - rw)
    r = rs[:, 0:w]
    k = rs[:, w:2 * w]
    v = rs[:, 2 * w:3 * w]
    wa = rs[:, 3 * w:3 * w + lora_in]
    gl = rs[:, 3 * w + lora_in:]

    z = -(w0_ref[...] + _dot(jnp.tanh(wa), wlora_ref[...]))
    w_log = -(jnp.maximum(z, 0.0) + jnp.log1p(jnp.exp(-jnp.abs(z)))) - 0.5
    lw = -jnp.exp(w_log)
    a = _sigmoid(a0_ref[...] + _dot(wa, alora_ref[...]))
    g = _dot(_sigmoid(gl), glora_ref[...])

    def head_sum(t):
        return _dot_sel(t, ones_ref[...], 2)

    kk = k * kk_ref[...]
    kk = kk * lax.rsqrt(head_sum(kk * kk) + KK_EPS)
    km = k * (1.0 + (a - 1.0) * ka_ref[...])
    b = kk * a

    cum = _sel_dot(tri_ref[...], lw, 3)
    tots = [cum[(s + 1) * c - 1:(s + 1) * c, :] for s in range(nb)]
    tot = jnp.concatenate([jnp.broadcast_to(t, (c, w)) for t in tots], axis=0)
    kk_t = kk * jnp.exp(cum - lw)
    r_t = r * jnp.exp(cum)
    inv = jnp.exp(-cum)
    k_h = km * inv
    b_h = b * inv
    tail = jnp.exp(tot - cum)
    k_p = km * tail
    b_p = b * tail
    gams = [jnp.exp(t) for t in tots]

    ti = lax.broadcasted_iota(jnp.int32, (c2, c2), 0)
    si = lax.broadcasted_iota(jnp.int32, (c2, c2), 1)
    same_head = (ti // c) == (si // c)
    strict = same_head & (si < ti)
    incl = same_head & (si <= ti)
    eye = (ti == si).astype(F32)
    keep = (lax.broadcasted_iota(jnp.int32, (c2, PAIR), 0) // c
            == lax.broadcasted_iota(jnp.int32, (c2, PAIR), 1) // RW_HEAD)

    def stack(t):
        out = []
        for s, p in chains:
            piece = t[s * c:(s + 1) * c, p * PAIR:(p + 1) * PAIR]
            out.append(jnp.where(keep, jnp.concatenate([piece, piece], axis=0), 0.0).astype(BF16))
        return out

    x_kk, x_r, y_k, y_b = stack(kk_t), stack(r_t), stack(k_h), stack(b_h)
    v_s, kp_s, bp_s = stack(v), stack(k_p), stack(b_p)
    lhs = [jnp.concatenate([x_kk[j], x_r[j]], axis=0) for j in range(n)]
    g_k = [_dot_nt(lhs[j], y_k[j]) for j in range(n)]
    g_b = [_dot_nt(lhs[j], y_b[j]) for j in range(n)]
    low = [jnp.where(strict, t[:c2], 0.0) for t in g_b]
    t_inv = [eye - jnp.where((ti // 2) == (si // 2), t, 0.0) for t in low]
    m = 2
    while m < c:
        sel = ((ti // (2 * m)) == (si // (2 * m))) & ((ti // m) != (si // m))
        half = [_dot(t_inv[j], jnp.where(sel, low[j], 0.0)) for j in range(n)]
        t_inv = [t_inv[j] - _dot(half[j], t_inv[j]) for j in range(n)]
        m *= 2
    w1 = [_dot(jnp.where(strict, g_k[j][:c2], 0.0), v_s[j]) for j in range(n)]
    pq = [_dot(t_inv[j], jnp.concatenate([x_kk[j], w1[j].astype(BF16)], axis=1)) for j in range(n)]
    s0 = [s_ref[s, p] for s, p in chains]
    u = [_dot_nt(pq[j][:, :PAIR], s0[j]) + pq[j][:, PAIR:] for j in range(n)]
    y = [_dot_nt(x_r[j], s0[j]) + _dot(jnp.where(incl, g_k[j][c2:], 0.0), v_s[j])
         - _dot(jnp.where(incl, g_b[j][c2:], 0.0), u[j]) for j in range(n)]
    for j, (s, p) in enumerate(chains):
        y_ref[s * c:(s + 1) * c, p * PAIR:(p + 1) * PAIR] = y[j][:c] + y[j][c:]
        s_ref[s, p] = (s0[j] * gams[s][:, p * PAIR:(p + 1) * PAIR]
                       + _dot_tn(v_s[j], kp_s[j]) - _dot_tn(u[j], bp_s[j]))

    y = y_ref[...]
    inv_n = 1.0 / RW_HEAD
    d = y - head_sum(y) * inv_n
    yn = d * lax.rsqrt(head_sum(d * d) * inv_n + LNX_EPS) * lnw_ref[...] + lnb_ref[...]
    bonus = head_sum(r * km * rk_ref[...]) * v
    o_ref[...] = ((yn + bonus) * g).reshape(nb, c, w)

    @pl.when(i == pl.num_programs(1) - 1)
    def _():
        for s, p in chains:
            wkv_out_ref[s, p] = _dot_sel(s_ref[s, p], coll_ref[...], 3)


def _rwkv(rw, shift_prev, wkv_prev, lp, consts, *, chunk, nb):
    bsz, t, sw = rw.shape
    w = lp["rw_w0"].shape[-1]
    n_pairs = w // PAIR
    lora_in = lp["wlora_pad"].shape[0]
    assert t % chunk == 0 and bsz % nb == 0 and sw == 3 * w + lora_in + lp["rw_g_lora"].shape[0]
    vec = lambda a: a.reshape(1, -1)
    full = lambda shape: pl.BlockSpec(shape, lambda b, i: (0,) * len(shape))
    tri, ones_blk, dup, coll = consts
    state_spec = pl.BlockSpec((nb, n_pairs, PAIR, RW_HEAD), lambda b, i: (b, 0, 0, 0))
    in_specs = [
        pl.BlockSpec((nb, chunk, sw), lambda b, i: (b, i, 0)),
        pl.BlockSpec((nb, 1, sw), lambda b, i: (b, 0, 0)),
        state_spec,
        full((1, sw)), full((1, w)), full((lora_in, w)), full((1, w)), full((lora_in, w)),
        full(lp["rw_g_lora"].shape), full((1, w)), full((1, w)), full((1, w)), full((1, w)), full((1, w)),
        full(tri.shape), full(ones_blk.shape), full(dup.shape), full(coll.shape),
    ]
    o, wkv_new = pl.pallas_call(
        functools.partial(_rwkv_kernel, lora_in=lora_in),
        grid=(bsz // nb, t // chunk),
        in_specs=in_specs,
        out_specs=[pl.BlockSpec((nb, chunk, w), lambda b, i: (b, i, 0)), state_spec],
        out_shape=[
            jax.ShapeDtypeStruct((bsz, t, w), F32),
            jax.ShapeDtypeStruct((bsz, n_pairs, PAIR, RW_HEAD), F32),
        ],
        scratch_shapes=[
            pltpu.VMEM((nb, n_pairs, PAIR, PAIR), F32),
            pltpu.VMEM((nb, 1, sw), F32),
            pltpu.VMEM((nb * chunk, w), F32),
        ],
        compiler_params=_params("parallel", "arbitrary"),
        name="rwkv",
    )(rw, shift_prev.reshape(bsz, 1, sw), wkv_prev.reshape(bsz, n_pairs, PAIR, RW_HEAD),
      vec(lp["rw_mu"]), vec(lp["rw_w0"]), lp["wlora_pad"], vec(lp["rw_a0"]), lp["alora_pad"],
      lp["rw_g_lora"], vec(lp["rw_k_k"]), vec(lp["rw_k_a"]), vec(lp["rw_r_k"]),
      vec(lp["rw_lnx_w"]), vec(lp["rw_lnx_b"]), tri, ones_blk, dup, coll)
    return o, wkv_new.reshape(bsz, w // RW_HEAD, RW_HEAD, RW_HEAD)


def _rwkv_consts(chunk, nb, w):
    idx = jnp.arange(nb * chunk)
    tri = ((idx[:, None] >= idx[None, :]) & (idx[:, None] // chunk == idx[None, :] // chunk)).astype(BF16)
    hid = jnp.arange(w) // RW_HEAD
    ones_blk = (hid[:, None] == hid[None, :]).astype(BF16)
    dup = (jnp.arange(RW_HEAD)[:, None] == (jnp.arange(PAIR) % RW_HEAD)[None, :]).astype(BF16)
    return tri, ones_blk, dup, dup.T


def _lambda(lamv_ref, lam_init):
    lamv = lamv_ref[...]
    l1 = jnp.sum(lamv[0:1] * lamv[1:2], axis=1, keepdims=True)
    l2 = jnp.sum(lamv[2:3] * lamv[3:4], axis=1, keepdims=True)
    return jnp.exp(l1) - jnp.exp(l2) + lam_init


def _flash_kernel(qt_ref, kt_ref, lamv_ref, g_ref, q_ref, k_ref, v_ref, o_ref, m_ref, a_ref, *, lam_init):
    step = pl.program_id(2)
    qi = qt_ref[step]
    ki = kt_ref[step]
    tq = q_ref.shape[0]
    tk = k_ref.shape[0]
    hb = q_ref.shape[1] // LANES
    n = 2 * hb

    @pl.when(ki == 0)
    def _():
        m_ref[...] = jnp.full(m_ref.shape, -jnp.inf, F32)
        a_ref[...] = jnp.zeros(a_ref.shape, F32)

    def update(masked):
        lane = lax.broadcasted_iota(jnp.int32, (1, LANES), 1)
        qs, ks, vs = [], [], []
        for h in range(hb):
            cs = slice(h * LANES, (h + 1) * LANES)
            q = q_ref[:, cs]
            k = k_ref[:, cs].astype(BF16)
            v = v_ref[:, cs].astype(BF16)
            v_ext = jnp.concatenate([v, jnp.ones_like(v)], axis=1)
            qs += [jnp.where(lane < DA_SUB, q, 0.0).astype(BF16), jnp.where(lane >= DA_SUB, q, 0.0).astype(BF16)]
            ks += [k, k]
            vs += [v_ext, v_ext]
        s = [lax.dot_general(qs[j], ks[j], NT_DIMS, preferred_element_type=F32) for j in range(n)]
        if masked:
            causal = (lax.broadcasted_iota(jnp.int32, (tq, tk), 1) <= lax.broadcasted_iota(jnp.int32, (tq, tk), 0))
            s = [jnp.where(causal, t, -jnp.inf) for t in s]
        m_prev = [m_ref[j] for j in range(n)]
        m_new = [jnp.maximum(m_prev[j], jnp.max(s[j], axis=1, keepdims=True)) for j in range(n)]
        alpha = [jnp.exp(m_prev[j] - m_new[j]) for j in range(n)]
        p = [jnp.exp(s[j] - jnp.tile(m_new[j], (1, tk // LANES))).astype(BF16) for j in range(n)]
        pv = [jnp.dot(p[j], vs[j], preferred_element_type=F32) for j in range(n)]
        for j in range(n):
            a_ref[j] = jnp.concatenate([alpha[j], alpha[j]], axis=1) * a_ref[j] + pv[j]
            m_ref[j] = m_new[j]

    @pl.when(ki < qi)
    def _():
        update(False)

    @pl.when(ki == qi)
    def _():
        update(True)
        lam = _lambda(lamv_ref, lam_init)
        for h in range(hb):
            a1 = a_ref[2 * h]
            a2 = a_ref[2 * h + 1]
            o = a1[:, :LANES] / a1[:, LANES:] - lam * (a2[:, :LANES] / a2[:, LANES:])
            o_ref[:, h * LANES:(h + 1) * LANES] = _rms(o, SUBLN_EPS) * g_ref[...] * (1.0 - lam_init)


def _flash_attention(qk, v, lamv, subln, *, bsz, t, lam_init, tile, hb):
    n, dw = v.shape
    heads = dw // LANES
    assert t % tile == 0 and n == bsz * t and heads % hb == 0
    nt = t // tile
    groups = heads // hb
    pairs = [(qi, ki) for qi in range(nt) for ki in range(qi + 1)]
    q_tab = jnp.array([p[0] for p in pairs], jnp.int32)
    k_tab = jnp.array([p[1] for p in pairs], jnp.int32)
    width = hb * LANES
    grid_spec = pltpu.PrefetchScalarGridSpec(
        num_scalar_prefetch=2,
        grid=(bsz, groups, len(pairs)),
        in_specs=[
            pl.BlockSpec(lamv.shape, lambda b, h, s, qt, kt: (0, 0)),
            pl.BlockSpec((1, LANES), lambda b, h, s, qt, kt: (0, 0)),
            pl.BlockSpec((tile, width), lambda b, h, s, qt, kt: (b * nt + qt[s], h)),
            pl.BlockSpec((tile, width), lambda b, h, s, qt, kt: (b * nt + kt[s], groups + h)),
            pl.BlockSpec((tile, width), lambda b, h, s, qt, kt: (b * nt + kt[s], h)),
        ],
        out_specs=pl.BlockSpec((tile, width), lambda b, h, s, qt, kt: (b * nt + qt[s], h)),
        scratch_shapes=[pltpu.VMEM((2 * hb, tile, LANES), F32), pltpu.VMEM((2 * hb, tile, 2 * LANES), F32)],
    )
    return pl.pallas_call(
        functools.partial(_flash_kernel, lam_init=lam_init),
        grid_spec=grid_spec,
        out_shape=jax.ShapeDtypeStruct((n, dw), F32),
        compiler_params=_params("parallel", "parallel", "arbitrary"),
        name="flash_diff_attention",
    )(q_tab, k_tab, lamv, subln.reshape(1, LANES), qk, qk, v)


def _paged_kernel(pt_ref, lamv_ref, g_ref, wq_ref, kn_ref, vn_ref, *rest, n_pages, lam_init):
    k_refs = rest[:n_pages]
    v_refs = rest[n_pages:2 * n_pages]
    o_ref = rest[2 * n_pages]
    tq = kn_ref.shape[0]
    heads = o_ref.shape[1] // LANES
    wq = wq_ref[...].astype(BF16)
    s_past = [jnp.dot(kr[...], wq, preferred_element_type=F32) for kr in k_refs]
    s_new = jnp.dot(kn_ref[...].astype(BF16), wq, preferred_element_type=F32)
    key_t = lax.broadcasted_iota(jnp.int32, s_new.shape, 0)
    qry_t = lax.broadcasted_iota(jnp.int32, s_new.shape, 1) % tq
    s_new = jnp.where(key_t <= qry_t, s_new, -jnp.inf)
    m_past = s_past[0]
    for s in s_past[1:]:
        m_past = jnp.maximum(m_past, s)
    m = jnp.maximum(jnp.max(m_past, axis=0, keepdims=True), jnp.max(s_new, axis=0, keepdims=True))
    p_past = [jnp.exp(s - m) for s in s_past]
    p_new = jnp.exp(s_new - m)
    acc = lax.dot_general(p_new.astype(BF16), vn_ref[...].astype(BF16), TN_DIMS, preferred_element_type=F32)
    p_sum = None
    for p, vr in zip(p_past, v_refs):
        acc = acc + lax.dot_general(p.astype(BF16), vr[...], TN_DIMS, preferred_element_type=F32)
        p_sum = p if p_sum is None else p_sum + p
    den = None
    for part, rows in ((p_sum, p_sum.shape[0]), (p_new, tq)):
        ones = jnp.ones((rows, LANES), BF16)
        for piece in _split(part, 2):
            term = lax.dot_general(piece, ones, TN_DIMS, preferred_element_type=F32)
            den = term if den is None else den + term
    lam = _lambda(lamv_ref, lam_init)
    for h in range(heads):
        r1 = slice(h * 2 * tq, h * 2 * tq + tq)
        r2 = slice(h * 2 * tq + tq, (h + 1) * 2 * tq)
        cs = slice(h * LANES, (h + 1) * LANES)
        o = acc[r1, cs] / den[r1, :] - lam * (acc[r2, cs] / den[r2, :])
        o_ref[:, cs] = _rms(o, SUBLN_EPS) * g_ref[...] * (1.0 - lam_init)


def _paged_attention(wq, k_new, v_new, cache_k, cache_v, page_table, lamv, subln, *, layer, lam_init):
    bsz, tq, dw = k_new.shape
    n_pages = page_table.shape[1]
    page = cache_k.shape[2]

    def page_spec(j):
        return pl.BlockSpec((None, None, page, dw), lambda b, pt: (layer, pt[b * n_pages + j], 0, 0))

    grid_spec = pltpu.PrefetchScalarGridSpec(
        num_scalar_prefetch=1,
        grid=(bsz,),
        in_specs=[
            pl.BlockSpec(lamv.shape, lambda b, pt: (0, 0)),
            pl.BlockSpec((1, LANES), lambda b, pt: (0, 0)),
            pl.BlockSpec((None,) + wq.shape[1:], lambda b, pt: (b, 0, 0)),
            pl.BlockSpec((None, tq, dw), lambda b, pt: (b, 0, 0)),
            pl.BlockSpec((None, tq, dw), lambda b, pt: (b, 0, 0)),
        ] + [page_spec(j) for j in range(n_pages)] * 2,
        out_specs=pl.BlockSpec((None, tq, dw), lambda b, pt: (b, 0, 0)),
    )
    return pl.pallas_call(
        functools.partial(_paged_kernel, n_pages=n_pages, lam_init=lam_init),
        grid_spec=grid_spec,
        out_shape=jax.ShapeDtypeStruct((bsz, tq, dw), F32),
        compiler_params=_params("parallel"),
        name="paged_diff_attention",
    )(page_table.reshape(-1), lamv, subln.reshape(1, LANES), wq, k_new, v_new,
      *([cache_k] * n_pages), *([cache_v] * n_pages))


def _merge_kernel(oa_ref, ob_ref, ga_ref, gb_ref, x_ref, wpa_ref, wpb_ref, wo_ref, g_ref, o_ref):
    m = ga_ref[...] * _dot(oa_ref[...], wpa_ref[...]) + gb_ref[...] * _dot(ob_ref[...], wpb_ref[...])
    a = _dot(m, wo_ref[...])
    o_ref[...] = x_ref[...] + _rms(a, NORM_EPS) * g_ref[...]


def _merge(o_a, o_b, gates, x, lp, *, tm):
    n, d = x.shape
    wd = o_a.shape[1]
    row = lambda width, j=0: pl.BlockSpec((tm, width), lambda i: (i, j))
    full = lambda shape: pl.BlockSpec(shape, lambda i: (0, 0))
    return pl.pallas_call(
        _merge_kernel,
        grid=(n // tm,),
        in_specs=[row(wd), row(wd), row(d, 0), row(d, 1), row(d),
                  full((wd, d)), full((wd, d)), full((d, d)), full((1, d))],
        out_specs=row(d),
        out_shape=jax.ShapeDtypeStruct((n, d), F32),
        compiler_params=_params("parallel"),
        name="merge",
    )(o_a, o_b, gates, gates, x, lp["w_pa"], lp["w_pb"], lp["w_o"], lp["norm_mix_post"].reshape(1, d))


def _ffn_kernel(*refs, seq_len, short):
    if short:
        x_ref, g1_ref, wu_ref, wz_ref, cw_ref, cb_ref, wd_ref, g2_ref, pad_ref, o_ref, u_ref, h_ref, acc_ref = refs
    else:
        (x_ref, g1_ref, wu_ref, wz_ref, cw_ref, cb_ref, wd_ref, g2_ref, cprev_ref,
         o_ref, last_ref, h_ref, acc_ref, carry_ref) = refs
    i = pl.program_id(0)
    c = pl.program_id(1)
    tm = x_ref.shape[0]

    @pl.when(c == 0)
    def _():
        h_ref[...] = (_rms(x_ref[...], NORM_EPS) * g1_ref[...]).astype(BF16)
        acc_ref[...] = jnp.zeros(acc_ref.shape, F32)

    h = h_ref[...]
    u = jnp.dot(h, wu_ref[...], preferred_element_type=F32)
    z = jnp.dot(h, wz_ref[...], preferred_element_type=F32)
    row = lax.broadcasted_iota(jnp.int32, (tm, 1), 0)
    if short:
        pad = pad_ref[...]
        t = row % seq_len
        s1 = jnp.where(t == 0, pltpu.roll(pad, tm - 1, 0), pltpu.roll(u, 1, 0))
        s2 = jnp.where(t < 2, pad, pltpu.roll(u, 2, 0))
        u_ref[...] = u
    else:
        @pl.when(i % (seq_len // tm) == 0)
        def _():
            carry_ref[c, pl.ds(SUBLANES - 2, 2), :] = cprev_ref[...]

        before = carry_ref[c, pl.ds(SUBLANES - 2, 2), :]
        p2 = before[0:1]
        p1 = before[1:2]
        s1 = jnp.where(row == 0, p1, pltpu.roll(u, 1, 0))
        s2 = jnp.where(row == 0, p2, jnp.where(row == 1, p1, pltpu.roll(u, 2, 0)))
        carry_ref[c] = u[tm - SUBLANES:tm]
        last_ref[...] = u[tm - 2:tm]
    cw = cw_ref[...]
    uc = cb_ref[...] + s2 * cw[0:1] + s1 * cw[1:2] + u * cw[2:3]
    act = 0.5 * uc * (1.0 + lax.erf(uc * (2.0 ** -0.5))) * z
    acc_ref[...] += jnp.dot(act.astype(BF16), wd_ref[...], preferred_element_type=F32)

    @pl.when(c == pl.num_programs(1) - 1)
    def _():
        o_ref[...] = x_ref[...] + _rms(acc_ref[...], NORM_EPS) * g2_ref[...]


def _ffn(x, conv_prev, lp, *, bsz, t, tm, tf):
    n, d = x.shape
    dff = lp["w_down"].shape[0]
    kw = lp["conv_w"].shape[0]
    assert kw == 3 and dff % tf == 0 and n % tm == 0 and t >= 2
    nc = dff // tf
    short = t < tm
    in_specs = [
        pl.BlockSpec((tm, d), lambda i, c: (i, 0)),
        pl.BlockSpec((1, d), lambda i, c: (0, 0)),
        pl.BlockSpec((d, tf), lambda i, c: (0, c)),
        pl.BlockSpec((d, tf), lambda i, c: (0, nc + c)),
        pl.BlockSpec((kw, tf), lambda i, c: (0, c)),
        pl.BlockSpec((1, tf), lambda i, c: (0, c)),
        pl.BlockSpec((tf, d), lambda i, c: (c, 0)),
        pl.BlockSpec((1, d), lambda i, c: (0, 0)),
    ]
    args = [x, lp["norm_ffn_pre"].reshape(1, d), lp["w_up"], lp["w_up"], lp["conv_w"],
            lp["conv_b"].reshape(1, dff), lp["w_down"], lp["norm_ffn_post"].reshape(1, d)]
    x_spec = pl.BlockSpec((tm, d), lambda i, c: (i, 0))
    scratch = [pltpu.VMEM((tm, d), BF16), pltpu.VMEM((tm, d), F32)]
    if short:
        assert tm % t == 0
        pad = jnp.concatenate([conv_prev, jnp.zeros((bsz, t - 2, dff), F32)], axis=1).reshape(n, dff)
        in_specs.append(pl.BlockSpec((tm, tf), lambda i, c: (i, c)))
        args.append(pad)
        out_specs = [x_spec, pl.BlockSpec((tm, tf), lambda i, c: (i, c))]
        out_shape = [jax.ShapeDtypeStruct((n, d), F32), jax.ShapeDtypeStruct((n, dff), F32)]
    else:
        assert t % tm == 0
        tiles = t // tm
        in_specs.append(pl.BlockSpec((None, 2, tf), lambda i, c: (i // tiles, 0, c)))
        args.append(conv_prev)
        out_specs = [x_spec, pl.BlockSpec((None, 2, tf), lambda i, c: (i, 0, c))]
        out_shape = [jax.ShapeDtypeStruct((n, d), F32), jax.ShapeDtypeStruct((n // tm, 2, dff), F32)]
        scratch.append(pltpu.VMEM((nc, SUBLANES, tf), F32))
    x_new, aux = pl.pallas_call(
        functools.partial(_ffn_kernel, seq_len=t, short=short),
        grid=(n // tm, nc),
        in_specs=in_specs,
        out_specs=out_specs,
        out_shape=out_shape,
        scratch_shapes=scratch,
        compiler_params=_params("arbitrary", "arbitrary"),
        name="ffn",
    )(*args)
    if short:
        conv_new = aux.reshape(bsz, t, dff)[:, t - 2:]
    else:
        conv_new = aux.reshape(bsz, t // tm, 2, dff)[:, -1]
    return x_new, conv_new


def _rope_tables(pos, rows, width):
    inv = jnp.exp(-math.log(ROPE_THETA) * jnp.arange(HALF_ROT, dtype=F32) * (2.0 / ROT_DIM))
    ang = pos.astype(F32)[:, None] * inv[None, :]
    cos, sin = jnp.cos(ang), jnp.sin(ang)
    t = pos.shape[0]
    zeros = jnp.zeros((t, HALF_ROT), F32)
    rest = DA_SUB - ROT_DIM
    cos_t = jnp.concatenate([cos, cos, jnp.ones((t, rest), F32)], axis=1)
    sup_t = jnp.concatenate([-sin, zeros, jnp.zeros((t, rest), F32)], axis=1)
    sdn_t = jnp.concatenate([zeros, sin, jnp.zeros((t, rest), F32)], axis=1)
    return tuple(jnp.tile(tab, (rows // t, width // DA_SUB)) for tab in (cos_t, sup_t, sdn_t))


def _tiles(n):
    return (512 if n >= 4096 else 256), 256


def _rwkv_blocking(bsz, t):
    chunk = min(t, 64)
    nb = 2 if chunk == 64 else 8
    return chunk, math.gcd(nb, bsz)


def _group_layer(x, bsz, t, lp, state, rope, consts, attention):
    shift_prev, wkv_prev, conv_prev = state
    tm, tf = _tiles(x.shape[0])
    chunk, nb = _rwkv_blocking(bsz, t)
    g = lp["norm_mix_pre"]
    rw = _norm_matmul(x, g, lp["w_rw"], tm=tm, tn=256)
    qk = _norm_matmul(x, g, lp["w_qk"], tm=tm, tn=rope[0].shape[1], epilogue="rope", rope=rope, n_q_tiles=1)
    v = _norm_matmul(x, g, lp["w_v"], tm=tm, tn=lp["w_v"].shape[1])
    gates = _norm_matmul(x, g, lp["w_g"], tm=tm, tn=512, epilogue="sigmoid")
    sw = rw.shape[1]
    o_a, wkv_new = _rwkv(rw.reshape(bsz, t, sw), shift_prev, wkv_prev, lp, consts, chunk=chunk, nb=nb)
    dw = v.shape[1]
    o_b = attention(qk, v)
    x = _merge(o_a.reshape(bsz * t, -1), o_b, gates, x, lp, tm=tm)
    x, conv_new = _ffn(x, conv_prev, lp, bsz=bsz, t=t, tm=tm, tf=tf)
    k_new = qk[:, dw:].reshape(bsz, t, dw // (2 * DA_SUB), 2, DA_SUB)
    v_new = v.reshape(bsz, t, dw // (2 * DA_SUB), 2 * DA_SUB)
    shift_new = rw.reshape(bsz, t, sw)[:, t - 1]
    return x, (k_new, v_new, wkv_new, shift_new, conv_new)


def kernel(x_prompt, x_sample, cache_k, cache_v, state_wkv, state_shift, state_conv, page_table, norm_mix_pre, norm_mix_post, norm_ffn_pre, norm_ffn_post, w_in, rw_mu, rw_w0, rw_w_lora, rw_a0, rw_a_lora, rw_g_lora, rw_k_k, rw_k_a, rw_r_k, rw_lnx_w, rw_lnx_b, da_lam_q1, da_lam_k1, da_lam_q2, da_lam_k2, da_subln, w_pa, w_pb, w_o, w_up, conv_w, conv_b, w_down):
    bp, tp, d = x_prompt.shape
    bs, ts, _ = x_sample.shape
    depth = w_in.shape[0]
    rww = rw_w0.shape[1]
    sw = rw_mu.shape[1]
    wl, al = rw_w_lora.shape[1], rw_a_lora.shape[1]
    dw = w_pb.shape[1]
    dff = w_down.shape[1]
    n_pages, page = page_table.shape[1], cache_k.shape[2]
    past = n_pages * page
    assert wl + al == LANES and sw == 3 * rww + wl + al + rw_g_lora.shape[1] and rww % PAIR == 0
    assert dw % LANES == 0 and w_in.shape[2] == sw + 3 * dw + 2 * d

    rope_p = _rope_tables(jnp.arange(tp, dtype=jnp.int32), tp, dw)
    tm_s, _ = _tiles(bs * ts)
    rope_s = _rope_tables(past + jnp.arange(ts, dtype=jnp.int32), tm_s, dw)
    consts_p = _rwkv_consts(*_rwkv_blocking(bp, tp), rww)
    consts_s = _rwkv_consts(*_rwkv_blocking(bs, ts), rww)
    cache_k4 = cache_k.reshape(depth, cache_k.shape[1], page, dw).astype(BF16)
    cache_v4 = cache_v.reshape(depth, cache_v.shape[1], page, dw).astype(BF16)
    sub_sel = (jnp.arange(dw)[:, None] // DA_SUB == jnp.arange(dw // DA_SUB)[None, :]).astype(F32)

    xp = x_prompt.reshape(bp * tp, d)
    xs = x_sample.reshape(bs * ts, d)
    state_p = (jnp.zeros((bp, sw), F32), jnp.zeros((bp, rww // RW_HEAD, RW_HEAD, RW_HEAD), F32),
               jnp.zeros((bp, 2, dff), F32))
    outs_p, outs_s = [], []
    for l in range(depth):
        wi = w_in[l].astype(BF16)
        lp = {
            "norm_mix_pre": norm_mix_pre[l], "norm_mix_post": norm_mix_post[l],
            "norm_ffn_pre": norm_ffn_pre[l], "norm_ffn_post": norm_ffn_post[l],
            "w_rw": wi[:, :sw], "w_qk": wi[:, sw:sw + 2 * dw], "w_v": wi[:, sw + 2 * dw:sw + 3 * dw],
            "w_g": wi[:, sw + 3 * dw:],
            "rw_mu": rw_mu[l], "rw_w0": rw_w0[l], "rw_a0": rw_a0[l],
            "wlora_pad": jnp.concatenate([rw_w_lora[l], jnp.zeros((al, rww), F32)], axis=0).astype(BF16),
            "alora_pad": jnp.concatenate([jnp.zeros((wl, rww), F32), rw_a_lora[l]], axis=0).astype(BF16),
            "rw_g_lora": rw_g_lora[l].astype(BF16),
            "rw_k_k": rw_k_k[l], "rw_k_a": rw_k_a[l], "rw_r_k": rw_r_k[l].reshape(-1),
            "rw_lnx_w": rw_lnx_w[l], "rw_lnx_b": rw_lnx_b[l],
            "w_pa": w_pa[l].astype(BF16), "w_pb": w_pb[l].astype(BF16), "w_o": w_o[l].astype(BF16),
            "w_up": w_up[l].astype(BF16), "conv_w": conv_w[l], "conv_b": conv_b[l],
            "w_down": w_down[l].astype(BF16),
        }
        lam_init = 0.8 - 0.6 * math.exp(-0.3 * l)
        lamv = jnp.stack([da_lam_q1[l], da_lam_k1[l], da_lam_q2[l], da_lam_k2[l]])
        subln = da_subln[l]

        def attn_prompt(qk, v):
            return _flash_attention(qk, v, lamv, subln, bsz=bp, t=tp, lam_init=lam_init, tile=min(tp, 512), hb=4)

        def attn_sample(qk, v):
            q = qk[:, :dw].reshape(bs, ts, dw)
            wq = (jnp.swapaxes(q, 1, 2)[:, :, None, :] * sub_sel[None, :, :, None]).reshape(bs, dw, -1)
            o = _paged_attention(wq, qk[:, dw:].reshape(bs, ts, dw), v.reshape(bs, ts, dw), cache_k4, cache_v4,
                                 page_table, lamv, subln, layer=l, lam_init=lam_init)
            return o.reshape(bs * ts, dw)

        xp, out_p = _group_layer(xp, bp, tp, lp, state_p, rope_p, consts_p, attn_prompt)
        xs, out_s = _group_layer(xs, bs, ts, lp, (state_shift[l], state_wkv[l], state_conv[l]), rope_s, consts_s,
                                 attn_sample)
        outs_p.append(out_p)
        outs_s.append(out_s)

    stack = lambda outs, j: jnp.stack([o[j] for o in outs])
    return ((xp.reshape(bp, tp, d), xs.reshape(bs, ts, d))
            + tuple(stack(outs_p, j) for j in range(5)) + tuple(stack(outs_s, j) for j in range(5)))
```

```python
import functools
import math

import jax
import jax.numpy as jnp
from jax import lax
from jax.experimental import pallas as pl
from jax.experimental.pallas import tpu as pltpu

F32 = jnp.float32
BF16 = jnp.bfloat16

RW_HEAD = 64
DA_SUB = 64
ROT_DIM = 16
ROPE_THETA = 500000.0
NORM_EPS = 1e-6
LNX_EPS = 64e-5
SUBLN_EPS = 1e-5
KK_EPS = 1e-12

LANES = 128
SUBLANES = 8
VMEM_LIMIT_BYTES = 48 * 1024 * 1024

PAIR = 2 * RW_HEAD
HALF_ROT = ROT_DIM // 2
TN_DIMS = (((0,), (0,)), ((), ()))
NT_DIMS = (((1,), (1,)), ((), ()))


def _params(*semantics):
    return pltpu.CompilerParams(dimension_semantics=semantics, vmem_limit_bytes=VMEM_LIMIT_BYTES)


def _sigmoid(x):
    return 1.0 / (1.0 + jnp.exp(-x))


def _rms(x, eps):
    return x * lax.rsqrt(jnp.mean(x * x, axis=-1, keepdims=True) + eps)


def _dot(a, b):
    return jnp.dot(a.astype(BF16), b.astype(BF16), preferred_element_type=F32)


def _dot_nt(a, b):
    return lax.dot_general(a.astype(BF16), b.astype(BF16), NT_DIMS, preferred_element_type=F32)


def _dot_tn(a, b):
    return lax.dot_general(a.astype(BF16), b.astype(BF16), TN_DIMS, preferred_element_type=F32)


def _split(x, parts):
    pieces = []
    rest = x
    for _ in range(parts):
        piece = rest.astype(BF16)
        pieces.append(piece)
        rest = rest - piece.astype(F32)
    return pieces


def _sel_dot(sel, x, parts):
    out = None
    for piece in _split(x, parts):
        term = jnp.dot(sel, piece, preferred_element_type=F32)
        out = term if out is None else out + term
    return out


def _dot_sel(x, sel, parts):
    out = None
    for piece in _split(x, parts):
        term = jnp.dot(piece, sel, preferred_element_type=F32)
        out = term if out is None else out + term
    return out


def _norm_matmul_kernel(*refs, epilogue, n_q_tiles):
    if epilogue == "rope":
        x_ref, g_ref, w_ref, cos_ref, sup_ref, sdn_ref, o_ref, h_ref = refs
    else:
        x_ref, g_ref, w_ref, o_ref, h_ref = refs
    j = pl.program_id(1)

    @pl.when(j == 0)
    def _():
        h_ref[...] = (_rms(x_ref[...], NORM_EPS) * g_ref[...]).astype(BF16)

    acc = jnp.dot(h_ref[...], w_ref[...], preferred_element_type=F32)
    if epilogue == "rope":
        tn = acc.shape[1]
        rot = (acc * cos_ref[...] + pltpu.roll(acc, HALF_ROT, 1) * sdn_ref[...]
               + pltpu.roll(acc, tn - HALF_ROT, 1) * sup_ref[...])
        o_ref[...] = rot * jnp.where(j < n_q_tiles, DA_SUB ** -0.5, 1.0)
    elif epilogue == "sigmoid":
        o_ref[...] = _sigmoid(acc)
    else:
        o_ref[...] = acc


def _norm_matmul(x, g, w, *, tm, tn, epilogue="none", rope=None, n_q_tiles=0):
    n, d = x.shape
    m = w.shape[1]
    assert n % tm == 0 and m % tn == 0
    in_specs = [
        pl.BlockSpec((tm, d), lambda i, j: (i, 0)),
        pl.BlockSpec((1, d), lambda i, j: (0, 0)),
        pl.BlockSpec((d, tn), lambda i, j: (0, j)),
    ]
    args = [x, g.reshape(1, d), w]
    if epilogue == "rope":
        tab_blocks = rope[0].shape[0] // tm
        assert rope[0].shape == (tab_blocks * tm, tn)
        in_specs += [pl.BlockSpec((tm, tn), lambda i, j: (i % tab_blocks, 0))] * 3
        args += list(rope)
    return pl.pallas_call(
        functools.partial(_norm_matmul_kernel, epilogue=epilogue, n_q_tiles=n_q_tiles),
        grid=(n // tm, m // tn),
        in_specs=in_specs,
        out_specs=pl.BlockSpec((tm, tn), lambda i, j: (i, j)),
        out_shape=jax.ShapeDtypeStruct((n, m), F32),
        scratch_shapes=[pltpu.VMEM((tm, d), BF16)],
        compiler_params=_params("parallel", "arbitrary"),
        name="norm_matmul_" + epilogue,
    )(*args)


def _rwkv_kernel(rw_ref, shift_ref, wkv_ref, mu_ref, w0_ref, wlora_ref, a0_ref, alora_ref, glora_ref,
                 kk_ref, ka_ref, rk_ref, lnw_ref, lnb_ref, tri_ref, ones_ref, dup_ref, coll_ref,
                 o_ref, wkv_out_ref, s_ref, prev_ref, y_ref, *, lora_in):
    i = pl.program_id(1)
    nb, c, sw = rw_ref.shape
    w = w0_ref.shape[1]
    n_pairs = w // PAIR
    c2 = 2 * c
    chains = [(s, p) for s in range(nb) for p in range(n_pairs)]
    n = len(chains)

    @pl.when(i == 0)
    def _():
        prev_ref[...] = shift_ref[...]
        blk = (lax.broadcasted_iota(jnp.int32, (PAIR, PAIR), 0) // RW_HEAD
               == lax.broadcasted_iota(jnp.int32, (PAIR, PAIR), 1) // RW_HEAD)
        for s, p in chains:
            s_ref[s, p] = jnp.where(blk, _dot_sel(wkv_ref[s, p], dup_ref[...], 3), 0.0)

    rw = rw_ref[...].reshape(nb * c, sw)
    row = lax.broadcasted_iota(jnp.int32, (nb * c, 1), 0)
    prev = pltpu.roll(rw, 1, 0)
    for s in range(nb):
        prev = jnp.where(row == s * c, prev_ref[s], prev)
        prev_ref[s] = rw[(s + 1) * c - 1:(s + 1) * c, :]
    rs = rw + mu_ref[...] * (prev - rw)
    r = rs[:, 0:w]
    k = rs[:, w:2 * w]
    v = rs[:, 2 * w:3 * w]
    wa = rs[:, 3 * w:3 * w + lora_in]
    gl = rs[:, 3 * w + lora_in:]

    z = -(w0_ref[...] + _dot(jnp.tanh(wa), wlora_ref[...]))
    w_log = -(jnp.maximum(z, 0.0) + jnp.log1p(jnp.exp(-jnp.abs(z)))) - 0.5
    lw = -jnp.exp(w_log)
    a = _sigmoid(a0_ref[...] + _dot(wa, alora_ref[...]))
    g = _dot(_sigmoid(gl), glora_ref[...])

    def head_sum(t):
        return _dot_sel(t, ones_ref[...], 2)

    kk = k * kk_ref[...]
    kk = kk * lax.rsqrt(head_sum(kk * kk) + KK_EPS)
    km = k * (1.0 + (a - 1.0) * ka_ref[...])
    b = kk * a

    cum = _sel_dot(tri_ref[...], lw, 3)
    tots = [cum[(s + 1) * c - 1:(s + 1) * c, :] for s in range(nb)]
    tot = jnp.concatenate([jnp.broadcast_to(t, (c, w)) for t in tots], axis=0)
    kk_t = kk * jnp.exp(cum - lw)
    r_t = r * jnp.exp(cum)
    inv = jnp.exp(-cum)
    k_h = km * inv
    b_h = b * inv
    tail = jnp.exp(tot - cum)
    k_p = km * tail
    b_p = b * tail
    gams = [jnp.exp(t) for t in tots]

    ti = lax.broadcasted_iota(jnp.int32, (c2, c2), 0)
    si = lax.broadcasted_iota(jnp.int32, (c2, c2), 1)
    same_head = (ti // c) == (si // c)
    strict = same_head & (si < ti)
    incl = same_head & (si <= ti)
    eye = (ti == si).astype(F32)
    keep = (lax.broadcasted_iota(jnp.int32, (c2, PAIR), 0) // c
            == lax.broadcasted_iota(jnp.int32, (c2, PAIR), 1) // RW_HEAD)

    def stack(t):
        out = []
        for s, p in chains:
            piece = t[s * c:(s + 1) * c, p * PAIR:(p + 1) * PAIR]
            out.append(jnp.where(keep, jnp.concatenate([piece, piece], axis=0), 0.0).astype(BF16))
        return out

    x_kk, x_r, y_k, y_b = stack(kk_t), stack(r_t), stack(k_h), stack(b_h)
    v_s, kp_s, bp_s = stack(v), stack(k_p), stack(b_p)
    lhs = [jnp.concatenate([x_kk[j], x_r[j]], axis=0) for j in range(n)]
    g_k = [_dot_nt(lhs[j], y_k[j]) for j in range(n)]
    g_b = [_dot_nt(lhs[j], y_b[j]) for j in range(n)]
    low = [jnp.where(strict, t[:c2], 0.0) for t in g_b]
    t_inv = [eye - jnp.where((ti // 2) == (si // 2), t, 0.0) for t in low]
    m = 2
    while m < c:
        sel = ((ti // (2 * m)) == (si // (2 * m))) & ((ti // m) != (si // m))
        half = [_dot(t_inv[j], jnp.where(sel, low[j], 0.0)) for j in range(n)]
        t_inv = [t_inv[j] - _dot(half[j], t_inv[j]) for j in range(n)]
        m *= 2
    w1 = [_dot(jnp.where(strict, g_k[j][:c2], 0.0), v_s[j]) for j in range(n)]
    pq = [_dot(t_inv[j], jnp.concatenate([x_kk[j], w1[j].astype(BF16)], axis=1)) for j in range(n)]
    s0 = [s_ref[s, p] for s, p in chains]
    u = [_dot_nt(pq[j][:, :PAIR], s0[j]) + pq[j][:, PAIR:] for j in range(n)]
    y = [_dot_nt(x_r[j], s0[j]) + _dot(jnp.where(incl, g_k[j][c2:], 0.0), v_s[j])
         - _dot(jnp.where(incl, g_b[j][c2:], 0.0), u[j]) for j in range(n)]
    for j, (s, p) in enumerate(chains):
        y_ref[s * c:(s + 1) * c, p * PAIR:(p + 1) * PAIR] = y[j][:c] + y[j][c:]
        s_ref[s, p] = (s0[j] * gams[s][:, p * PAIR:(p + 1) * PAIR]
                       + _dot_tn(v_s[j], kp_s[j]) - _dot_tn(u[j], bp_s[j]))

    y = y_ref[...]
    inv_n = 1.0 / RW_HEAD
    d = y - head_sum(y) * inv_n
    yn = d * lax.rsqrt(head_sum(d * d) * inv_n + LNX_EPS) * lnw_ref[...] + lnb_ref[...]
    bonus = head_sum(r * km * rk_ref[...]) * v
    o_ref[...] = ((yn + bonus) * g).reshape(nb, c, w)

    @pl.when(i == pl.num_programs(1) - 1)
    def _():
        for s, p in chains:
            wkv_out_ref[s, p] = _dot_sel(s_ref[s, p], coll_ref[...], 3)


def _rwkv(rw, shift_prev, wkv_prev, lp, consts, *, chunk, nb):
    bsz, t, sw = rw.shape
    w = lp["rw_w0"].shape[-1]
    n_pairs = w // PAIR
    lora_in = lp["wlora_pad"].shape[0]
    assert t % chunk == 0 and bsz % nb == 0 and sw == 3 * w + lora_in + lp["rw_g_lora"].shape[0]
    vec = lambda a: a.reshape(1, -1)
    full = lambda shape: pl.BlockSpec(shape, lambda b, i: (0,) * len(shape))
    tri, ones_blk, dup, coll = consts
    state_spec = pl.BlockSpec((nb, n_pairs, PAIR, RW_HEAD), lambda b, i: (b, 0, 0, 0))
    in_specs = [
        pl.BlockSpec((nb, chunk, sw), lambda b, i: (b, i, 0)),
        pl.BlockSpec((nb, 1, sw), lambda b, i: (b, 0, 0)),
        state_spec,
        full((1, sw)), full((1, w)), full((lora_in, w)), full((1, w)), full((lora_in, w)),
        full(lp["rw_g_lora"].shape), full((1, w)), full((1, w)), full((1, w)), full((1, w)), full((1, w)),
        full(tri.shape), full(ones_blk.shape), full(dup.shape), full(coll.shape),
    ]
    o, wkv_new = pl.pallas_call(
        functools.partial(_rwkv_kernel, lora_in=lora_in),
        grid=(bsz // nb, t // chunk),
        in_specs=in_specs,
        out_specs=[pl.BlockSpec((nb, chunk, w), lambda b, i: (b, i, 0)), state_spec],
        out_shape=[
            jax.ShapeDtypeStruct((bsz, t, w), F32),
            jax.ShapeDtypeStruct((bsz, n_pairs, PAIR, RW_HEAD), F32),
        ],
        scratch_shapes=[
            pltpu.VMEM((nb, n_pairs, PAIR, PAIR), F32),
            pltpu.VMEM((nb, 1, sw), F32),
            pltpu.VMEM((nb * chunk, w), F32),
        ],
        compiler_params=_params("parallel", "arbitrary"),
        name="rwkv",
    )(rw, shift_prev.reshape(bsz, 1, sw), wkv_prev.reshape(bsz, n_pairs, PAIR, RW_HEAD),
      vec(lp["rw_mu"]), vec(lp["rw_w0"]), lp["wlora_pad"], vec(lp["rw_a0"]), lp["alora_pad"],
      lp["rw_g_lora"], vec(lp["rw_k_k"]), vec(lp["rw_k_a"]), vec(lp["rw_r_k"]),
      vec(lp["rw_lnx_w"]), vec(lp["rw_lnx_b"]), tri, ones_blk, dup, coll)
    return o, wkv_new.reshape(bsz, w // RW_HEAD, RW_HEAD, RW_HEAD)


def _rwkv_consts(chunk, nb, w):
    idx = jnp.arange(nb * chunk)
    tri = ((idx[:, None] >= idx[None, :]) & (idx[:, None] // chunk == idx[None, :] // chunk)).astype(BF16)
    hid = jnp.arange(w) // RW_HEAD
    ones_blk = (hid[:, None] == hid[None, :]).astype(BF16)
    dup = (jnp.arange(RW_HEAD)[:, None] == (jnp.arange(PAIR) % RW_HEAD)[None, :]).astype(BF16)
    return tri, ones_blk, dup, dup.T


def _lambda(lamv_ref, lam_init):
    lamv = lamv_ref[...]
    l1 = jnp.sum(lamv[0:1] * lamv[1:2], axis=1, keepdims=True)
    l2 = jnp.sum(lamv[2:3] * lamv[3:4], axis=1, keepdims=True)
    return jnp.exp(l1) - jnp.exp(l2) + lam_init


def _flash_kernel(qt_ref, kt_ref, lamv_ref, g_ref, q_ref, k_ref, v_ref, o_ref, m_ref, a_ref, *, lam_init):
    step = pl.program_id(2)
    qi = qt_ref[step]
    ki = kt_ref[step]
    tq = q_ref.shape[0]
    tk = k_ref.shape[0]
    hb = q_ref.shape[1] // LANES
    n = 2 * hb

    @pl.when(ki == 0)
    def _():
        m_ref[...] = jnp.full(m_ref.shape, -jnp.inf, F32)
        a_ref[...] = jnp.zeros(a_ref.shape, F32)

    def update(masked):
        lane = lax.broadcasted_iota(jnp.int32, (1, LANES), 1)
        qs, ks, vs = [], [], []
        for h in range(hb):
            cs = slice(h * LANES, (h + 1) * LANES)
            q = q_ref[:, cs]
            k = k_ref[:, cs].astype(BF16)
            v = v_ref[:, cs].astype(BF16)
            v_ext = jnp.concatenate([v, jnp.ones_like(v)], axis=1)
            qs += [jnp.where(lane < DA_SUB, q, 0.0).astype(BF16), jnp.where(lane >= DA_SUB, q, 0.0).astype(BF16)]
            ks += [k, k]
            vs += [v_ext, v_ext]
        s = [lax.dot_general(qs[j], ks[j], NT_DIMS, preferred_element_type=F32) for j in range(n)]
        if masked:
            causal = (lax.broadcasted_iota(jnp.int32, (tq, tk), 1) <= lax.broadcasted_iota(jnp.int32, (tq, tk), 0))
            s = [jnp.where(causal, t, -jnp.inf) for t in s]
        m_prev = [m_ref[j] for j in range(n)]
        m_new = [jnp.maximum(m_prev[j], jnp.max(s[j], axis=1, keepdims=True)) for j in range(n)]
        alpha = [jnp.exp(m_prev[j] - m_new[j]) for j in range(n)]
        p = [jnp.exp(s[j] - jnp.tile(m_new[j], (1, tk // LANES))).astype(BF16) for j in range(n)]
        pv = [jnp.dot(p[j], vs[j], preferred_element_type=F32) for j in range(n)]
        for j in range(n):
            a_ref[j] = jnp.concatenate([alpha[j], alpha[j]], axis=1) * a_ref[j] + pv[j]
            m_ref[j] = m_new[j]

    @pl.when(ki < qi)
    def _():
        update(False)

    @pl.when(ki == qi)
    def _():
        update(True)
        lam = _lambda(lamv_ref, lam_init)
        for h in range(hb):
            a1 = a_ref[2 * h]
            a2 = a_ref[2 * h + 1]
            o = a1[:, :LANES] / a1[:, LANES:] - lam * (a2[:, :LANES] / a2[:, LANES:])
            o_ref[:, h * LANES:(h + 1) * LANES] = _rms(o, SUBLN_EPS) * g_ref[...] * (1.0 - lam_init)


def _flash_attention(qk, v, lamv, subln, *, bsz, t, lam_init, tile, hb):
    n, dw = v.shape
    heads = dw // LANES
    assert t % tile == 0 and n == bsz * t and heads % hb == 0
    nt = t // tile
    groups = heads // hb
    pairs = [(qi, ki) for qi in range(nt) for ki in range(qi + 1)]
    q_tab = jnp.array([p[0] for p in pairs], jnp.int32)
    k_tab = jnp.array([p[1] for p in pairs], jnp.int32)
    width = hb * LANES
    grid_spec = pltpu.PrefetchScalarGridSpec(
        num_scalar_prefetch=2,
        grid=(bsz, groups, len(pairs)),
        in_specs=[
            pl.BlockSpec(lamv.shape, lambda b, h, s, qt, kt: (0, 0)),
            pl.BlockSpec((1, LANES), lambda b, h, s, qt, kt: (0, 0)),
            pl.BlockSpec((tile, width), lambda b, h, s, qt, kt: (b * nt + qt[s], h)),
            pl.BlockSpec((tile, width), lambda b, h, s, qt, kt: (b * nt + kt[s], groups + h)),
            pl.BlockSpec((tile, width), lambda b, h, s, qt, kt: (b * nt + kt[s], h)),
        ],
        out_specs=pl.BlockSpec((tile, width), lambda b, h, s, qt, kt: (b * nt + qt[s], h)),
        scratch_shapes=[pltpu.VMEM((2 * hb, tile, LANES), F32), pltpu.VMEM((2 * hb, tile, 2 * LANES), F32)],
    )
    return pl.pallas_call(
        functools.partial(_flash_kernel, lam_init=lam_init),
        grid_spec=grid_spec,
        out_shape=jax.ShapeDtypeStruct((n, dw), F32),
        compiler_params=_params("parallel", "parallel", "arbitrary"),
        name="flash_diff_attention",
    )(q_tab, k_tab, lamv, subln.reshape(1, LANES), qk, qk, v)


def _paged_kernel(pt_ref, lamv_ref, g_ref, wq_ref, kn_ref, vn_ref, *rest, n_pages, lam_init):
    kt_refs = rest[:n_pages]
    v_refs = rest[n_pages:2 * n_pages]
    o_ref = rest[2 * n_pages]
    tq = kn_ref.shape[0]
    heads = o_ref.shape[1] // LANES
    page = kt_refs[0].shape[1]
    rows = 2 * tq
    wq = wq_ref[...].astype(BF16)
    s_past = [jnp.dot(wq, kt[...].astype(BF16), preferred_element_type=F32) for kt in kt_refs]
    s_new = lax.dot_general(wq, kn_ref[...].astype(BF16), NT_DIMS, preferred_element_type=F32)
    qry_t = lax.broadcasted_iota(jnp.int32, s_new.shape, 0) % tq
    key_t = lax.broadcasted_iota(jnp.int32, s_new.shape, 1)
    s_new = jnp.where(key_t <= qry_t, s_new, -jnp.inf)
    m_past = s_past[0]
    for s in s_past[1:]:
        m_past = jnp.maximum(m_past, s)
    m = jnp.maximum(jnp.max(m_past, axis=1, keepdims=True), jnp.max(s_new, axis=1, keepdims=True))
    p_past = [jnp.exp(s - m) for s in s_past]
    p_new = jnp.exp(s_new - m)
    p_sum = p_past[0]
    for p in p_past[1:]:
        p_sum = p_sum + p
    den = jnp.sum(p_sum, axis=1, keepdims=True) + jnp.sum(p_new, axis=1, keepdims=True)
    p_past = [p.astype(BF16) for p in p_past]
    p_new = p_new.astype(BF16)
    vn = vn_ref[...].astype(BF16)
    lam = _lambda(lamv_ref, lam_init)
    for h in range(heads):
        rs = slice(h * rows, (h + 1) * rows)
        cs = slice(h * LANES, (h + 1) * LANES)
        acc = jnp.dot(p_new[rs], vn[:, cs], preferred_element_type=F32)
        for p, vr in zip(p_past, v_refs):
            v_h = vr[pl.ds(h, page, stride=heads), :]
            acc = acc + jnp.dot(p[rs], v_h.astype(BF16), preferred_element_type=F32)
        acc = acc / den[rs]
        o = acc[:tq] - lam * acc[tq:]
        o_ref[:, cs] = _rms(o, SUBLN_EPS) * g_ref[...] * (1.0 - lam_init)


def _paged_attention(wq, k_new, v_new, cache_kt, cache_v, page_table, lamv, subln, *, layer, lam_init):
    bsz, tq, dw = k_new.shape
    n_pages = page_table.shape[1]

    def page_spec(shape, j):
        return pl.BlockSpec((None, None) + shape, lambda b, pt: (layer, pt[b * n_pages + j]) + (0,) * len(shape))

    grid_spec = pltpu.PrefetchScalarGridSpec(
        num_scalar_prefetch=1,
        grid=(bsz,),
        in_specs=[
            pl.BlockSpec(lamv.shape, lambda b, pt: (0, 0)),
            pl.BlockSpec((1, LANES), lambda b, pt: (0, 0)),
            pl.BlockSpec((None,) + wq.shape[1:], lambda b, pt: (b, 0, 0)),
            pl.BlockSpec((None, tq, dw), lambda b, pt: (b, 0, 0)),
            pl.BlockSpec((None, tq, dw), lambda b, pt: (b, 0, 0)),
        ] + [page_spec(cache_kt.shape[2:], j) for j in range(n_pages)]
          + [page_spec(cache_v.shape[2:], j) for j in range(n_pages)],
        out_specs=pl.BlockSpec((None, tq, dw), lambda b, pt: (b, 0, 0)),
    )
    return pl.pallas_call(
        functools.partial(_paged_kernel, n_pages=n_pages, lam_init=lam_init),
        grid_spec=grid_spec,
        out_shape=jax.ShapeDtypeStruct((bsz, tq, dw), F32),
        compiler_params=_params("parallel"),
        name="paged_diff_attention",
    )(page_table.reshape(-1), lamv, subln.reshape(1, LANES), wq, k_new, v_new,
      *([cache_kt] * n_pages), *([cache_v] * n_pages))


def _merge_kernel(oa_ref, ob_ref, ga_ref, gb_ref, x_ref, wpa_ref, wpb_ref, wo_ref, g_ref, o_ref):
    m = ga_ref[...] * _dot(oa_ref[...], wpa_ref[...]) + gb_ref[...] * _dot(ob_ref[...], wpb_ref[...])
    a = _dot(m, wo_ref[...])
    o_ref[...] = x_ref[...] + _rms(a, NORM_EPS) * g_ref[...]


def _merge(o_a, o_b, gates, x, lp, *, tm):
    n, d = x.shape
    wd = o_a.shape[1]
    row = lambda width, j=0: pl.BlockSpec((tm, width), lambda i: (i, j))
    full = lambda shape: pl.BlockSpec(shape, lambda i: (0, 0))
    return pl.pallas_call(
        _merge_kernel,
        grid=(n // tm,),
        in_specs=[row(wd), row(wd), row(d, 0), row(d, 1), row(d),
                  full((wd, d)), full((wd, d)), full((d, d)), full((1, d))],
        out_specs=row(d),
        out_shape=jax.ShapeDtypeStruct((n, d), F32),
        compiler_params=_params("parallel"),
        name="merge",
    )(o_a, o_b, gates, gates, x, lp["w_pa"], lp["w_pb"], lp["w_o"], lp["norm_mix_post"].reshape(1, d))


def _ffn_kernel(*refs, seq_len, short):
    if short:
        x_ref, g1_ref, wu_ref, wz_ref, cw_ref, cb_ref, wd_ref, g2_ref, pad_ref, o_ref, u_ref, h_ref, acc_ref = refs
    else:
        (x_ref, g1_ref, wu_ref, wz_ref, cw_ref, cb_ref, wd_ref, g2_ref, cprev_ref,
         o_ref, last_ref, h_ref, acc_ref, carry_ref) = refs
    i = pl.program_id(0)
    c = pl.program_id(1)
    tm = x_ref.shape[0]

    @pl.when(c == 0)
    def _():
        h_ref[...] = (_rms(x_ref[...], NORM_EPS) * g1_ref[...]).astype(BF16)
        acc_ref[...] = jnp.zeros(acc_ref.shape, F32)

    h = h_ref[...]
    u = jnp.dot(h, wu_ref[...], preferred_element_type=F32)
    z = jnp.dot(h, wz_ref[...], preferred_element_type=F32)
    row = lax.broadcasted_iota(jnp.int32, (tm, 1), 0)
    if short:
        pad = pad_ref[...]
        t = row % seq_len
        s1 = jnp.where(t == 0, pltpu.roll(pad, tm - 1, 0), pltpu.roll(u, 1, 0))
        s2 = jnp.where(t < 2, pad, pltpu.roll(u, 2, 0))
        u_ref[...] = u
    else:
        @pl.when(i % (seq_len // tm) == 0)
        def _():
            carry_ref[c, pl.ds(SUBLANES - 2, 2), :] = cprev_ref[...]

        before = carry_ref[c, pl.ds(SUBLANES - 2, 2), :]
        p2 = before[0:1]
        p1 = before[1:2]
        s1 = jnp.where(row == 0, p1, pltpu.roll(u, 1, 0))
        s2 = jnp.where(row == 0, p2, jnp.where(row == 1, p1, pltpu.roll(u, 2, 0)))
        carry_ref[c] = u[tm - SUBLANES:tm]
        last_ref[...] = u[tm - 2:tm]
    cw = cw_ref[...]
    uc = cb_ref[...] + s2 * cw[0:1] + s1 * cw[1:2] + u * cw[2:3]
    act = 0.5 * uc * (1.0 + lax.erf(uc * (2.0 ** -0.5))) * z
    acc_ref[...] += jnp.dot(act.astype(BF16), wd_ref[...], preferred_element_type=F32)

    @pl.when(c == pl.num_programs(1) - 1)
    def _():
        o_ref[...] = x_ref[...] + _rms(acc_ref[...], NORM_EPS) * g2_ref[...]


def _ffn(x, conv_prev, lp, *, bsz, t, tm, tf):
    n, d = x.shape
    dff = lp["w_down"].shape[0]
    kw = lp["conv_w"].shape[0]
    assert kw == 3 and dff % tf == 0 and n % tm == 0 and t >= 2
    nc = dff // tf
    short = t < tm
    in_specs = [
        pl.BlockSpec((tm, d), lambda i, c: (i, 0)),
        pl.BlockSpec((1, d), lambda i, c: (0, 0)),
        pl.BlockSpec((d, tf), lambda i, c: (0, c)),
        pl.BlockSpec((d, tf), lambda i, c: (0, nc + c)),
        pl.BlockSpec((kw, tf), lambda i, c: (0, c)),
        pl.BlockSpec((1, tf), lambda i, c: (0, c)),
        pl.BlockSpec((tf, d), lambda i, c: (c, 0)),
        pl.BlockSpec((1, d), lambda i, c: (0, 0)),
    ]
    args = [x, lp["norm_ffn_pre"].reshape(1, d), lp["w_up"], lp["w_up"], lp["conv_w"],
            lp["conv_b"].reshape(1, dff), lp["w_down"], lp["norm_ffn_post"].reshape(1, d)]
    x_spec = pl.BlockSpec((tm, d), lambda i, c: (i, 0))
    scratch = [pltpu.VMEM((tm, d), BF16), pltpu.VMEM((tm, d), F32)]
    if short:
        assert tm % t == 0
        pad = jnp.concatenate([conv_prev, jnp.zeros((bsz, t - 2, dff), F32)], axis=1).reshape(n, dff)
        in_specs.append(pl.BlockSpec((tm, tf), lambda i, c: (i, c)))
        args.append(pad)
        out_specs = [x_spec, pl.BlockSpec((tm, tf), lambda i, c: (i, c))]
        out_shape = [jax.ShapeDtypeStruct((n, d), F32), jax.ShapeDtypeStruct((n, dff), F32)]
    else:
        assert t % tm == 0
        tiles = t // tm
        in_specs.append(pl.BlockSpec((None, 2, tf), lambda i, c: (i // tiles, 0, c)))
        args.append(conv_prev)
        out_specs = [x_spec, pl.BlockSpec((None, 2, tf), lambda i, c: (i, 0, c))]
        out_shape = [jax.ShapeDtypeStruct((n, d), F32), jax.ShapeDtypeStruct((n // tm, 2, dff), F32)]
        scratch.append(pltpu.VMEM((nc, SUBLANES, tf), F32))
    x_new, aux = pl.pallas_call(
        functools.partial(_ffn_kernel, seq_len=t, short=short),
        grid=(n // tm, nc),
        in_specs=in_specs,
        out_specs=out_specs,
        out_shape=out_shape,
        scratch_shapes=scratch,
        compiler_params=_params("arbitrary", "arbitrary"),
        name="ffn",
    )(*args)
    if short:
        conv_new = aux.reshape(bsz, t, dff)[:, t - 2:]
    else:
        conv_new = aux.reshape(bsz, t // tm, 2, dff)[:, -1]
    return x_new, conv_new


def _rope_tables(pos, rows, width):
    inv = jnp.exp(-math.log(ROPE_THETA) * jnp.arange(HALF_ROT, dtype=F32) * (2.0 / ROT_DIM))
    ang = pos.astype(F32)[:, None] * inv[None, :]
    cos, sin = jnp.cos(ang), jnp.sin(ang)
    t = pos.shape[0]
    zeros = jnp.zeros((t, HALF_ROT), F32)
    rest = DA_SUB - ROT_DIM
    cos_t = jnp.concatenate([cos, cos, jnp.ones((t, rest), F32)], axis=1)
    sup_t = jnp.concatenate([-sin, zeros, jnp.zeros((t, rest), F32)], axis=1)
    sdn_t = jnp.concatenate([zeros, sin, jnp.zeros((t, rest), F32)], axis=1)
    return tuple(jnp.tile(tab, (rows // t, width // DA_SUB)) for tab in (cos_t, sup_t, sdn_t))


def _tiles(n, t):
    tm = next(c for c in (1024, 512, 256, 128) if n % c == 0 and (t % c == 0 or c % t == 0))
    return tm, min(tm, 512), 256


def _col_tile(m, cap):
    return max(c for c in range(LANES, cap + 1, LANES) if m % c == 0)


def _rwkv_blocking(bsz, t):
    chunk = min(t, 64)
    nb = 2 if chunk == 64 else 8
    return chunk, math.gcd(nb, bsz)


def _group_layer(x, bsz, t, lp, state, rope, consts, attention):
    shift_prev, wkv_prev, conv_prev = state
    tm, tm_merge, tf = _tiles(x.shape[0], t)
    chunk, nb = _rwkv_blocking(bsz, t)
    g = lp["norm_mix_pre"]
    rw = _norm_matmul(x, g, lp["w_rw"], tm=tm, tn=_col_tile(lp["w_rw"].shape[1], 1024))
    qk = _norm_matmul(x, g, lp["w_qk"], tm=tm, tn=rope[0].shape[1], epilogue="rope", rope=rope, n_q_tiles=1)
    v = _norm_matmul(x, g, lp["w_v"], tm=tm, tn=lp["w_v"].shape[1])
    gates = _norm_matmul(x, g, lp["w_g"], tm=tm, tn=_col_tile(lp["w_g"].shape[1], 1024), epilogue="sigmoid")
    sw = rw.shape[1]
    o_a, wkv_new = _rwkv(rw.reshape(bsz, t, sw), shift_prev, wkv_prev, lp, consts, chunk=chunk, nb=nb)
    dw = v.shape[1]
    o_b = attention(qk, v)
    x = _merge(o_a.reshape(bsz * t, -1), o_b, gates, x, lp, tm=tm_merge)
    x, conv_new = _ffn(x, conv_prev, lp, bsz=bsz, t=t, tm=tm, tf=tf)
    k_new = qk[:, dw:].reshape(bsz, t, dw // (2 * DA_SUB), 2, DA_SUB)
    v_new = v.reshape(bsz, t, dw // (2 * DA_SUB), 2 * DA_SUB)
    shift_new = rw.reshape(bsz, t, sw)[:, t - 1]
    return x, (k_new, v_new, wkv_new, shift_new, conv_new)


def kernel(x_prompt, x_sample, cache_k, cache_v, state_wkv, state_shift, state_conv, page_table, norm_mix_pre, norm_mix_post, norm_ffn_pre, norm_ffn_post, w_in, rw_mu, rw_w0, rw_w_lora, rw_a0, rw_a_lora, rw_g_lora, rw_k_k, rw_k_a, rw_r_k, rw_lnx_w, rw_lnx_b, da_lam_q1, da_lam_k1, da_lam_q2, da_lam_k2, da_subln, w_pa, w_pb, w_o, w_up, conv_w, conv_b, w_down):
    bp, tp, d = x_prompt.shape
    bs, ts, _ = x_sample.shape
    depth = w_in.shape[0]
    rww = rw_w0.shape[1]
    sw = rw_mu.shape[1]
    wl, al = rw_w_lora.shape[1], rw_a_lora.shape[1]
    dw = w_pb.shape[1]
    dff = w_down.shape[1]
    n_pages, page = page_table.shape[1], cache_k.shape[2]
    past = n_pages * page
    assert wl + al == LANES and sw == 3 * rww + wl + al + rw_g_lora.shape[1] and rww % PAIR == 0
    assert dw % LANES == 0 and w_in.shape[2] == sw + 3 * dw + 2 * d

    rope_p = _rope_tables(jnp.arange(tp, dtype=jnp.int32), tp, dw)
    tm_s = _tiles(bs * ts, ts)[0]
    rope_s = _rope_tables(past + jnp.arange(ts, dtype=jnp.int32), tm_s, dw)
    consts_p = _rwkv_consts(*_rwkv_blocking(bp, tp), rww)
    consts_s = _rwkv_consts(*_rwkv_blocking(bs, ts), rww)
    cache_kt = jnp.transpose(cache_k, (0, 1, 3, 4, 5, 2)).reshape(depth, cache_k.shape[1], dw, page)
    cache_vr = cache_v.reshape(depth, cache_v.shape[1], page * (dw // LANES), LANES)
    sub_sel = (jnp.arange(dw // DA_SUB)[:, None] == jnp.arange(dw)[None, :] // DA_SUB).astype(F32)

    xp = x_prompt.reshape(bp * tp, d)
    xs = x_sample.reshape(bs * ts, d)
    state_p = (jnp.zeros((bp, sw), F32), jnp.zeros((bp, rww // RW_HEAD, RW_HEAD, RW_HEAD), F32),
               jnp.zeros((bp, 2, dff), F32))
    outs_p, outs_s = [], []
    for l in range(depth):
        wi = w_in[l].astype(BF16)
        lp = {
            "norm_mix_pre": norm_mix_pre[l], "norm_mix_post": norm_mix_post[l],
            "norm_ffn_pre": norm_ffn_pre[l], "norm_ffn_post": norm_ffn_post[l],
            "w_rw": wi[:, :sw], "w_qk": wi[:, sw:sw + 2 * dw], "w_v": wi[:, sw + 2 * dw:sw + 3 * dw],
            "w_g": wi[:, sw + 3 * dw:],
            "rw_mu": rw_mu[l], "rw_w0": rw_w0[l], "rw_a0": rw_a0[l],
            "wlora_pad": jnp.concatenate([rw_w_lora[l], jnp.zeros((al, rww), F32)], axis=0).astype(BF16),
            "alora_pad": jnp.concatenate([jnp.zeros((wl, rww), F32), rw_a_lora[l]], axis=0).astype(BF16),
            "rw_g_lora": rw_g_lora[l].astype(BF16),
            "rw_k_k": rw_k_k[l], "rw_k_a": rw_k_a[l], "rw_r_k": rw_r_k[l].reshape(-1),
            "rw_lnx_w": rw_lnx_w[l], "rw_lnx_b": rw_lnx_b[l],
            "w_pa": w_pa[l].astype(BF16), "w_pb": w_pb[l].astype(BF16), "w_o": w_o[l].astype(BF16),
            "w_up": w_up[l].astype(BF16), "conv_w": conv_w[l], "conv_b": conv_b[l],
            "w_down": w_down[l].astype(BF16),
        }
        lam_init = 0.8 - 0.6 * math.exp(-0.3 * l)
        lamv = jnp.stack([da_lam_q1[l], da_lam_k1[l], da_lam_q2[l], da_lam_k2[l]])
        subln = da_subln[l]

        def attn_prompt(qk, v):
            return _flash_attention(qk, v, lamv, subln, bsz=bp, t=tp, lam_init=lam_init, tile=min(tp, 512), hb=4)

        def attn_sample(qk, v):
            q = qk[:, :dw].reshape(bs, ts, dw)
            wq = (q[:, None, :, :] * sub_sel[None, :, None, :]).reshape(bs, -1, dw)
            o = _paged_attention(wq, qk[:, dw:].reshape(bs, ts, dw), v.reshape(bs, ts, dw), cache_kt, cache_vr,
                                 page_table, lamv, subln, layer=l, lam_init=lam_init)
            return o.reshape(bs * ts, dw)

        xp, out_p = _group_layer(xp, bp, tp, lp, state_p, rope_p, consts_p, attn_prompt)
        xs, out_s = _group_layer(xs, bs, ts, lp, (state_shift[l], state_wkv[l], state_conv[l]), rope_s, consts_s,
                                 attn_sample)
        outs_p.append(out_p)
        outs_s.append(out_s)

    stack = lambda outs, j: jnp.stack([o[j] for o in outs])
    return ((xp.reshape(bp, tp, d), xs.reshape(bs, ts, d))
            + tuple(stack(outs_p, j) for j in range(5)) + tuple(stack(outs_s, j) for j in range(5)))
```

```python
import functools
import math

import jax
import jax.numpy as jnp
from jax import lax
from jax.experimental import pallas as pl
from jax.experimental.pallas import tpu as pltpu

F32 = jnp.float32
BF16 = jnp.bfloat16

RW_HEAD = 64
DA_SUB = 64
ROT_DIM = 16
ROPE_THETA = 500000.0
NORM_EPS = 1e-6
LNX_EPS = 64e-5
SUBLN_EPS = 1e-5
KK_EPS = 1e-12

LANES = 128
SUBLANES = 8
VMEM_LIMIT_BYTES = 48 * 1024 * 1024

PAIR = 2 * RW_HEAD
HALF_ROT = ROT_DIM // 2
TN_DIMS = (((0,), (0,)), ((), ()))
NT_DIMS = (((1,), (1,)), ((), ()))


def _params(*semantics):
    return pltpu.CompilerParams(dimension_semantics=semantics, vmem_limit_bytes=VMEM_LIMIT_BYTES)


def _sigmoid(x):
    return 1.0 / (1.0 + jnp.exp(-x))


def _rms(x, eps):
    return x * lax.rsqrt(jnp.mean(x * x, axis=-1, keepdims=True) + eps)


def _dot(a, b):
    return jnp.dot(a.astype(BF16), b.astype(BF16), preferred_element_type=F32)


def _dot_nt(a, b):
    return lax.dot_general(a.astype(BF16), b.astype(BF16), NT_DIMS, preferred_element_type=F32)


def _dot_tn(a, b):
    return lax.dot_general(a.astype(BF16), b.astype(BF16), TN_DIMS, preferred_element_type=F32)


def _split(x, parts):
    pieces = []
    rest = x
    for _ in range(parts):
        piece = rest.astype(BF16)
        pieces.append(piece)
        rest = rest - piece.astype(F32)
    return pieces


def _sel_dot(sel, x, parts):
    out = None
    for piece in _split(x, parts):
        term = jnp.dot(sel, piece, preferred_element_type=F32)
        out = term if out is None else out + term
    return out


def _dot_sel(x, sel, parts):
    out = None
    for piece in _split(x, parts):
        term = jnp.dot(piece, sel, preferred_element_type=F32)
        out = term if out is None else out + term
    return out


def _norm_matmul_kernel(x_ref, g_ref, w_ref, o_ref, h_ref, *, epilogue):
    j = pl.program_id(1)

    @pl.when(j == 0)
    def _():
        h_ref[...] = (_rms(x_ref[...], NORM_EPS) * g_ref[...]).astype(BF16)

    acc = jnp.dot(h_ref[...], w_ref[...], preferred_element_type=F32)
    o_ref[...] = _sigmoid(acc) if epilogue == "sigmoid" else acc


def _norm_matmul(x, g, w, *, tm, tn, epilogue="none"):
    n, d = x.shape
    m = w.shape[1]
    assert n % tm == 0 and m % tn == 0
    return pl.pallas_call(
        functools.partial(_norm_matmul_kernel, epilogue=epilogue),
        grid=(n // tm, m // tn),
        in_specs=[
            pl.BlockSpec((tm, d), lambda i, j: (i, 0)),
            pl.BlockSpec((1, d), lambda i, j: (0, 0)),
            pl.BlockSpec((d, tn), lambda i, j: (0, j)),
        ],
        out_specs=pl.BlockSpec((tm, tn), lambda i, j: (i, j)),
        out_shape=jax.ShapeDtypeStruct((n, m), F32),
        scratch_shapes=[pltpu.VMEM((tm, d), BF16)],
        compiler_params=_params("parallel", "arbitrary"),
        name="norm_matmul_" + epilogue,
    )(x, g.reshape(1, d), w)


def _qkv_kernel(x_ref, g_ref, w_ref, cos_ref, sup_ref, sdn_ref, q_ref, k_ref, v_ref):
    dw = q_ref.shape[1]
    h = (_rms(x_ref[...], NORM_EPS) * g_ref[...]).astype(BF16)

    def rope(t):
        return (t * cos_ref[...] + pltpu.roll(t, HALF_ROT, 1) * sdn_ref[...]
                + pltpu.roll(t, dw - HALF_ROT, 1) * sup_ref[...])

    q = jnp.dot(h, w_ref[:, 0:dw], preferred_element_type=F32)
    k = jnp.dot(h, w_ref[:, dw:2 * dw], preferred_element_type=F32)
    v_ref[...] = jnp.dot(h, w_ref[:, 2 * dw:3 * dw], preferred_element_type=F32)
    q_ref[...] = rope(q) * (DA_SUB ** -0.5)
    k_ref[...] = rope(k)


def _qkv(x, g, w, rope, *, tm):
    n, d = x.shape
    dw = w.shape[1] // 3
    tab_blocks = rope[0].shape[0] // tm
    assert n % tm == 0 and rope[0].shape == (tab_blocks * tm, dw)
    row = pl.BlockSpec((tm, dw), lambda i: (i, 0))
    return pl.pallas_call(
        _qkv_kernel,
        grid=(n // tm,),
        in_specs=[
            pl.BlockSpec((tm, d), lambda i: (i, 0)),
            pl.BlockSpec((1, d), lambda i: (0, 0)),
            pl.BlockSpec((d, 3 * dw), lambda i: (0, 0)),
        ] + [pl.BlockSpec((tm, dw), lambda i: (i % tab_blocks, 0))] * 3,
        out_specs=[row, row, row],
        out_shape=[jax.ShapeDtypeStruct((n, dw), F32)] * 3,
        compiler_params=_params("parallel"),
        name="qkv_rope",
    )(x, g.reshape(1, d), w, *rope)


def _rwkv_kernel(rw_ref, shift_ref, wkv_ref, mu_ref, w0_ref, wlora_ref, a0_ref, alora_ref, glora_ref,
                 kk_ref, ka_ref, rk_ref, lnw_ref, lnb_ref, tri_ref, ones_ref, dup_ref, coll_ref,
                 o_ref, wkv_out_ref, s_ref, prev_ref, y_ref, *, lora_in):
    i = pl.program_id(1)
    nb, c, sw = rw_ref.shape
    w = w0_ref.shape[1]
    n_pairs = w // PAIR
    c2 = 2 * c
    chains = [(s, p) for s in range(nb) for p in range(n_pairs)]
    n = len(chains)

    @pl.when(i == 0)
    def _():
        prev_ref[...] = shift_ref[...]
        blk = (lax.broadcasted_iota(jnp.int32, (PAIR, PAIR), 0) // RW_HEAD
               == lax.broadcasted_iota(jnp.int32, (PAIR, PAIR), 1) // RW_HEAD)
        for s, p in chains:
            s_ref[s, p] = jnp.where(blk, _dot_sel(wkv_ref[s, p], dup_ref[...], 3), 0.0)

    rw = rw_ref[...].reshape(nb * c, sw)
    row = lax.broadcasted_iota(jnp.int32, (nb * c, 1), 0)
    prev = pltpu.roll(rw, 1, 0)
    for s in range(nb):
        prev = jnp.where(row == s * c, prev_ref[s], prev)
        prev_ref[s] = rw[(s + 1) * c - 1:(s + 1) * c, :]
    rs = rw + mu_ref[...] * (prev - rw)
    r = rs[:, 0:w]
    k = rs[:, w:2 * w]
    v = rs[:, 2 * w:3 * w]
    wa = rs[:, 3 * w:3 * w + lora_in]
    gl = rs[:, 3 * w + lora_in:]

    z = -(w0_ref[...] + _dot(jnp.tanh(wa), wlora_ref[...]))
    w_log = -(jnp.maximum(z, 0.0) + jnp.log1p(jnp.exp(-jnp.abs(z)))) - 0.5
    lw = -jnp.exp(w_log)
    a = _sigmoid(a0_ref[...] + _dot(wa, alora_ref[...]))
    g = _dot(_sigmoid(gl), glora_ref[...])

    def head_sum(t):
        return _dot_sel(t, ones_ref[...], 1)

    kk = k * kk_ref[...]
    kk = kk * lax.rsqrt(head_sum(kk * kk) + KK_EPS)
    km = k * (1.0 + (a - 1.0) * ka_ref[...])
    b = kk * a

    cum = _sel_dot(tri_ref[...], lw, 3)
    tots = [cum[(s + 1) * c - 1:(s + 1) * c, :] for s in range(nb)]
    tot = jnp.concatenate([jnp.broadcast_to(t, (c, w)) for t in tots], axis=0)
    kk_t = kk * jnp.exp(cum - lw)
    r_t = r * jnp.exp(cum)
    inv = jnp.exp(-cum)
    k_h = km * inv
    b_h = b * inv
    tail = jnp.exp(tot - cum)
    k_p = km * tail
    b_p = b * tail
    gams = [jnp.exp(t) for t in tots]

    tt = lax.broadcasted_iota(jnp.int32, (c, c2), 0)
    ss = lax.broadcasted_iota(jnp.int32, (c, c2), 1) % c
    strict = ss < tt
    incl = ss <= tt
    eye = (ss == tt).astype(F32)
    keep_lane = (lax.broadcasted_iota(jnp.int32, (c2, PAIR), 0) // c
                 == lax.broadcasted_iota(jnp.int32, (c2, PAIR), 1) // RW_HEAD)
    keep_time = (lax.broadcasted_iota(jnp.int32, (c2, c2), 0) // c
                 == lax.broadcasted_iota(jnp.int32, (c2, c2), 1) // c)
    blk = (lax.broadcasted_iota(jnp.int32, (PAIR, PAIR), 0) // RW_HEAD
           == lax.broadcasted_iota(jnp.int32, (PAIR, PAIR), 1) // RW_HEAD)

    def stack(t, keep):
        return jnp.where(keep, jnp.concatenate([t, t], axis=0), 0.0).astype(BF16)

    def pieces(t):
        return [t[s * c:(s + 1) * c, p * PAIR:(p + 1) * PAIR] for s, p in chains]

    kk_n, r_n, v_n = pieces(kk_t), pieces(r_t), pieces(v)
    kp_n, bp_n = pieces(k_p), pieces(b_p)
    y_k = [stack(t, keep_lane) for t in pieces(k_h)]
    y_b = [stack(t, keep_lane) for t in pieces(b_h)]
    v_s = [stack(t, keep_lane) for t in v_n]
    lhs = [jnp.concatenate([kk_n[j], r_n[j]], axis=0) for j in range(n)]
    g_k = [_dot_nt(lhs[j], y_k[j]) for j in range(n)]
    g_b = [_dot_nt(lhs[j], y_b[j]) for j in range(n)]
    low = [jnp.where(strict, t[:c], 0.0) for t in g_b]
    t_inv = [eye - jnp.where((tt // 2) == (ss // 2), t, 0.0) for t in low]
    m = 2
    while m < c:
        sel = ((tt // (2 * m)) == (ss // (2 * m))) & ((tt // m) != (ss // m))
        half = [_dot(t_inv[j], stack(jnp.where(sel, low[j], 0.0), keep_time)) for j in range(n)]
        t_inv = [t_inv[j] - _dot(half[j], stack(t_inv[j], keep_time)) for j in range(n)]
        m *= 2
    w1 = [_dot(jnp.where(strict, g_k[j][:c], 0.0), v_s[j]) for j in range(n)]
    pq = [_dot(t_inv[j], jnp.concatenate([stack(kk_n[j], keep_lane), stack(w1[j], keep_lane)], axis=1))
          for j in range(n)]
    s0 = [s_ref[s, p] for s, p in chains]
    u = [_dot_nt(pq[j][:, :PAIR], s0[j]) + pq[j][:, PAIR:] for j in range(n)]
    y = [_dot_nt(r_n[j], s0[j]) + _dot(jnp.where(incl, g_k[j][c:], 0.0), v_s[j])
         - _dot(jnp.where(incl, g_b[j][c:], 0.0), stack(u[j], keep_lane)) for j in range(n)]
    for j, (s, p) in enumerate(chains):
        y_ref[s * c:(s + 1) * c, p * PAIR:(p + 1) * PAIR] = y[j]
        vu = jnp.concatenate([v_n[j], u[j]], axis=0)
        kb = jnp.concatenate([kp_n[j], -bp_n[j]], axis=0)
        s_ref[s, p] = s0[j] * gams[s][:, p * PAIR:(p + 1) * PAIR] + jnp.where(blk, _dot_tn(vu, kb), 0.0)

    y = y_ref[...]
    inv_n = 1.0 / RW_HEAD
    d = y - head_sum(y) * inv_n
    yn = d * lax.rsqrt(head_sum(d * d) * inv_n + LNX_EPS) * lnw_ref[...] + lnb_ref[...]
    bonus = head_sum(r * km * rk_ref[...]) * v
    o_ref[...] = ((yn + bonus) * g).reshape(nb, c, w)

    @pl.when(i == pl.num_programs(1) - 1)
    def _():
        for s, p in chains:
            wkv_out_ref[s, p] = _dot_sel(s_ref[s, p], coll_ref[...], 3)


def _rwkv(rw, shift_prev, wkv_prev, lp, consts, *, chunk, nb):
    bsz, t, sw = rw.shape
    w = lp["rw_w0"].shape[-1]
    n_pairs = w // PAIR
    lora_in = lp["wlora_pad"].shape[0]
    assert t % chunk == 0 and bsz % nb == 0 and sw == 3 * w + lora_in + lp["rw_g_lora"].shape[0]
    vec = lambda a: a.reshape(1, -1)
    full = lambda shape: pl.BlockSpec(shape, lambda b, i: (0,) * len(shape))
    tri, ones_blk, dup, coll = consts
    state_spec = pl.BlockSpec((nb, n_pairs, PAIR, RW_HEAD), lambda b, i: (b, 0, 0, 0))
    in_specs = [
        pl.BlockSpec((nb, chunk, sw), lambda b, i: (b, i, 0)),
        pl.BlockSpec((nb, 1, sw), lambda b, i: (b, 0, 0)),
        state_spec,
        full((1, sw)), full((1, w)), full((lora_in, w)), full((1, w)), full((lora_in, w)),
        full(lp["rw_g_lora"].shape), full((1, w)), full((1, w)), full((1, w)), full((1, w)), full((1, w)),
        full(tri.shape), full(ones_blk.shape), full(dup.shape), full(coll.shape),
    ]
    o, wkv_new = pl.pallas_call(
        functools.partial(_rwkv_kernel, lora_in=lora_in),
        grid=(bsz // nb, t // chunk),
        in_specs=in_specs,
        out_specs=[pl.BlockSpec((nb, chunk, w), lambda b, i: (b, i, 0)), state_spec],
        out_shape=[
            jax.ShapeDtypeStruct((bsz, t, w), F32),
            jax.ShapeDtypeStruct((bsz, n_pairs, PAIR, RW_HEAD), F32),
        ],
        scratch_shapes=[
            pltpu.VMEM((nb, n_pairs, PAIR, PAIR), F32),
            pltpu.VMEM((nb, 1, sw), F32),
            pltpu.VMEM((nb * chunk, w), F32),
        ],
        compiler_params=_params("parallel", "arbitrary"),
        name="rwkv",
    )(rw, shift_prev.reshape(bsz, 1, sw), wkv_prev.reshape(bsz, n_pairs, PAIR, RW_HEAD),
      vec(lp["rw_mu"]), vec(lp["rw_w0"]), lp["wlora_pad"], vec(lp["rw_a0"]), lp["alora_pad"],
      lp["rw_g_lora"], vec(lp["rw_k_k"]), vec(lp["rw_k_a"]), vec(lp["rw_r_k"]),
      vec(lp["rw_lnx_w"]), vec(lp["rw_lnx_b"]), tri, ones_blk, dup, coll)
    return o, wkv_new.reshape(bsz, w // RW_HEAD, RW_HEAD, RW_HEAD)


def _rwkv_consts(chunk, nb, w):
    idx = jnp.arange(nb * chunk)
    tri = ((idx[:, None] >= idx[None, :]) & (idx[:, None] // chunk == idx[None, :] // chunk)).astype(BF16)
    hid = jnp.arange(w) // RW_HEAD
    ones_blk = (hid[:, None] == hid[None, :]).astype(BF16)
    dup = (jnp.arange(RW_HEAD)[:, None] == (jnp.arange(PAIR) % RW_HEAD)[None, :]).astype(BF16)
    return tri, ones_blk, dup, dup.T


def _lambda(lamv_ref, lam_init):
    lamv = lamv_ref[...]
    l1 = jnp.sum(lamv[0:1] * lamv[1:2], axis=1, keepdims=True)
    l2 = jnp.sum(lamv[2:3] * lamv[3:4], axis=1, keepdims=True)
    return jnp.exp(l1) - jnp.exp(l2) + lam_init


def _flash_kernel(qt_ref, kt_ref, lamv_ref, g_ref, q_ref, k_ref, v_ref, o_ref, m_ref, a_ref, *, lam_init):
    step = pl.program_id(2)
    qi = qt_ref[step]
    ki = kt_ref[step]
    tq = q_ref.shape[0]
    tk = k_ref.shape[0]
    hb = q_ref.shape[1] // LANES
    n = 2 * hb

    @pl.when(ki == 0)
    def _():
        m_ref[...] = jnp.full(m_ref.shape, -jnp.inf, F32)
        a_ref[...] = jnp.zeros(a_ref.shape, F32)

    def update(masked):
        lane = lax.broadcasted_iota(jnp.int32, (1, LANES), 1)
        qs, ks, vs = [], [], []
        for h in range(hb):
            cs = slice(h * LANES, (h + 1) * LANES)
            q = q_ref[:, cs]
            k = k_ref[:, cs].astype(BF16)
            v = v_ref[:, cs].astype(BF16)
            v_ext = jnp.concatenate([v, jnp.ones_like(v)], axis=1)
            qs += [jnp.where(lane < DA_SUB, q, 0.0).astype(BF16), jnp.where(lane >= DA_SUB, q, 0.0).astype(BF16)]
            ks += [k, k]
            vs += [v_ext, v_ext]
        s = [lax.dot_general(qs[j], ks[j], NT_DIMS, preferred_element_type=F32) for j in range(n)]
        if masked:
            causal = (lax.broadcasted_iota(jnp.int32, (tq, tk), 1) <= lax.broadcasted_iota(jnp.int32, (tq, tk), 0))
            s = [jnp.where(causal, t, -jnp.inf) for t in s]
        m_prev = [m_ref[j] for j in range(n)]
        m_new = [jnp.maximum(m_prev[j], jnp.max(s[j], axis=1, keepdims=True)) for j in range(n)]
        alpha = [jnp.exp(m_prev[j] - m_new[j]) for j in range(n)]
        p = [jnp.exp(s[j] - jnp.tile(m_new[j], (1, tk // LANES))).astype(BF16) for j in range(n)]
        pv = [jnp.dot(p[j], vs[j], preferred_element_type=F32) for j in range(n)]
        for j in range(n):
            a_ref[j] = jnp.concatenate([alpha[j], alpha[j]], axis=1) * a_ref[j] + pv[j]
            m_ref[j] = m_new[j]

    @pl.when(ki < qi)
    def _():
        update(False)

    @pl.when(ki == qi)
    def _():
        update(True)
        lam = _lambda(lamv_ref, lam_init)
        for h in range(hb):
            a1 = a_ref[2 * h]
            a2 = a_ref[2 * h + 1]
            o = a1[:, :LANES] / a1[:, LANES:] - lam * (a2[:, :LANES] / a2[:, LANES:])
            o_ref[:, h * LANES:(h + 1) * LANES] = _rms(o, SUBLN_EPS) * g_ref[...] * (1.0 - lam_init)


def _flash_attention(q, k, v, lamv, subln, *, bsz, t, lam_init, tile, hb):
    n, dw = v.shape
    heads = dw // LANES
    assert t % tile == 0 and n == bsz * t and heads % hb == 0
    nt = t // tile
    groups = heads // hb
    pairs = [(qi, ki) for qi in range(nt) for ki in range(qi + 1)]
    q_tab = jnp.array([p[0] for p in pairs], jnp.int32)
    k_tab = jnp.array([p[1] for p in pairs], jnp.int32)
    width = hb * LANES
    grid_spec = pltpu.PrefetchScalarGridSpec(
        num_scalar_prefetch=2,
        grid=(bsz, groups, len(pairs)),
        in_specs=[
            pl.BlockSpec(lamv.shape, lambda b, h, s, qt, kt: (0, 0)),
            pl.BlockSpec((1, LANES), lambda b, h, s, qt, kt: (0, 0)),
            pl.BlockSpec((tile, width), lambda b, h, s, qt, kt: (b * nt + qt[s], h)),
            pl.BlockSpec((tile, width), lambda b, h, s, qt, kt: (b * nt + kt[s], h)),
            pl.BlockSpec((tile, width), lambda b, h, s, qt, kt: (b * nt + kt[s], h)),
        ],
        out_specs=pl.BlockSpec((tile, width), lambda b, h, s, qt, kt: (b * nt + qt[s], h)),
        scratch_shapes=[pltpu.VMEM((2 * hb, tile, LANES), F32), pltpu.VMEM((2 * hb, tile, 2 * LANES), F32)],
    )
    return pl.pallas_call(
        functools.partial(_flash_kernel, lam_init=lam_init),
        grid_spec=grid_spec,
        out_shape=jax.ShapeDtypeStruct((n, dw), F32),
        compiler_params=_params("parallel", "parallel", "arbitrary"),
        name="flash_diff_attention",
    )(q_tab, k_tab, lamv, subln.reshape(1, LANES), q, k, v)


def _paged_kernel(pt_ref, lamv_ref, g_ref, wq_ref, kn_ref, vn_ref, *rest, n_pages, lam_init):
    kt_refs = rest[:n_pages]
    v_refs = rest[n_pages:2 * n_pages]
    o_ref = rest[2 * n_pages]
    tq = kn_ref.shape[0]
    heads = o_ref.shape[1] // LANES
    page = kt_refs[0].shape[1]
    rows = 2 * tq
    wq = wq_ref[...].astype(BF16)
    s_past = [jnp.dot(wq, kt[...].astype(BF16), preferred_element_type=F32) for kt in kt_refs]
    s_new = lax.dot_general(wq, kn_ref[...].astype(BF16), NT_DIMS, preferred_element_type=F32)
    qry_t = lax.broadcasted_iota(jnp.int32, s_new.shape, 0) % tq
    key_t = lax.broadcasted_iota(jnp.int32, s_new.shape, 1)
    s_new = jnp.where(key_t <= qry_t, s_new, -jnp.inf)
    m_past = s_past[0]
    for s in s_past[1:]:
        m_past = jnp.maximum(m_past, s)
    m = jnp.maximum(jnp.max(m_past, axis=1, keepdims=True), jnp.max(s_new, axis=1, keepdims=True))
    p_past = [jnp.exp(s - m) for s in s_past]
    p_new = jnp.exp(s_new - m)
    p_sum = p_past[0]
    for p in p_past[1:]:
        p_sum = p_sum + p
    den = jnp.sum(p_sum, axis=1, keepdims=True) + jnp.sum(p_new, axis=1, keepdims=True)
    p_past = [p.astype(BF16) for p in p_past]
    p_new = p_new.astype(BF16)
    vn = vn_ref[...].astype(BF16)
    lam = _lambda(lamv_ref, lam_init)
    for h in range(heads):
        rs = slice(h * rows, (h + 1) * rows)
        cs = slice(h * LANES, (h + 1) * LANES)
        acc = jnp.dot(p_new[rs], vn[:, cs], preferred_element_type=F32)
        for p, vr in zip(p_past, v_refs):
            v_h = vr[pl.ds(h, page, stride=heads), :]
            acc = acc + jnp.dot(p[rs], v_h.astype(BF16), preferred_element_type=F32)
        acc = acc / den[rs]
        o = acc[:tq] - lam * acc[tq:]
        o_ref[:, cs] = _rms(o, SUBLN_EPS) * g_ref[...] * (1.0 - lam_init)


def _paged_attention(wq, k_new, v_new, cache_kt, cache_v, page_table, lamv, subln, *, layer, lam_init):
    bsz, tq, dw = k_new.shape
    n_pages = page_table.shape[1]

    def page_spec(shape, j):
        return pl.BlockSpec((None, None) + shape, lambda b, pt: (layer, pt[b * n_pages + j]) + (0,) * len(shape))

    grid_spec = pltpu.PrefetchScalarGridSpec(
        num_scalar_prefetch=1,
        grid=(bsz,),
        in_specs=[
            pl.BlockSpec(lamv.shape, lambda b, pt: (0, 0)),
            pl.BlockSpec((1, LANES), lambda b, pt: (0, 0)),
            pl.BlockSpec((None,) + wq.shape[1:], lambda b, pt: (b, 0, 0)),
            pl.BlockSpec((None, tq, dw), lambda b, pt: (b, 0, 0)),
            pl.BlockSpec((None, tq, dw), lambda b, pt: (b, 0, 0)),
        ] + [page_spec(cache_kt.shape[2:], j) for j in range(n_pages)]
          + [page_spec(cache_v.shape[2:], j) for j in range(n_pages)],
        out_specs=pl.BlockSpec((None, tq, dw), lambda b, pt: (b, 0, 0)),
    )
    return pl.pallas_call(
        functools.partial(_paged_kernel, n_pages=n_pages, lam_init=lam_init),
        grid_spec=grid_spec,
        out_shape=jax.ShapeDtypeStruct((bsz, tq, dw), F32),
        compiler_params=_params("parallel"),
        name="paged_diff_attention",
    )(page_table.reshape(-1), lamv, subln.reshape(1, LANES), wq, k_new, v_new,
      *([cache_kt] * n_pages), *([cache_v] * n_pages))


def _merge_kernel(oa_ref, ob_ref, ga_ref, gb_ref, x_ref, wpa_ref, wpb_ref, wo_ref, g_ref, o_ref):
    m = ga_ref[...] * _dot(oa_ref[...], wpa_ref[...]) + gb_ref[...] * _dot(ob_ref[...], wpb_ref[...])
    a = _dot(m, wo_ref[...])
    o_ref[...] = x_ref[...] + _rms(a, NORM_EPS) * g_ref[...]


def _merge(o_a, o_b, gates, x, lp, *, tm):
    n, d = x.shape
    wd = o_a.shape[1]
    row = lambda width, j=0: pl.BlockSpec((tm, width), lambda i: (i, j))
    full = lambda shape: pl.BlockSpec(shape, lambda i: (0, 0))
    return pl.pallas_call(
        _merge_kernel,
        grid=(n // tm,),
        in_specs=[row(wd), row(wd), row(d, 0), row(d, 1), row(d),
                  full((wd, d)), full((wd, d)), full((d, d)), full((1, d))],
        out_specs=row(d),
        out_shape=jax.ShapeDtypeStruct((n, d), F32),
        compiler_params=_params("parallel"),
        name="merge",
    )(o_a, o_b, gates, gates, x, lp["w_pa"], lp["w_pb"], lp["w_o"], lp["norm_mix_post"].reshape(1, d))


def _ffn_kernel(*refs, seq_len, short):
    if short:
        x_ref, g1_ref, wu_ref, wz_ref, cw_ref, cb_ref, wd_ref, g2_ref, pad_ref, o_ref, u_ref, h_ref, acc_ref = refs
    else:
        (x_ref, g1_ref, wu_ref, wz_ref, cw_ref, cb_ref, wd_ref, g2_ref, cprev_ref,
         o_ref, last_ref, h_ref, acc_ref, carry_ref) = refs
    i = pl.program_id(0)
    c = pl.program_id(1)
    tm = x_ref.shape[0]

    @pl.when(c == 0)
    def _():
        h_ref[...] = (_rms(x_ref[...], NORM_EPS) * g1_ref[...]).astype(BF16)
        acc_ref[...] = jnp.zeros(acc_ref.shape, F32)

    h = h_ref[...]
    u = jnp.dot(h, wu_ref[...], preferred_element_type=F32)
    z = jnp.dot(h, wz_ref[...], preferred_element_type=F32)
    row = lax.broadcasted_iota(jnp.int32, (tm, 1), 0)
    if short:
        pad = pad_ref[...]
        t = row % seq_len
        s1 = jnp.where(t == 0, pltpu.roll(pad, tm - 1, 0), pltpu.roll(u, 1, 0))
        s2 = jnp.where(t < 2, pad, pltpu.roll(u, 2, 0))
        u_ref[...] = u
    else:
        @pl.when(i % (seq_len // tm) == 0)
        def _():
            carry_ref[c, pl.ds(SUBLANES - 2, 2), :] = cprev_ref[...]

        before = carry_ref[c, pl.ds(SUBLANES - 2, 2), :]
        p2 = before[0:1]
        p1 = before[1:2]
        s1 = jnp.where(row == 0, p1, pltpu.roll(u, 1, 0))
        s2 = jnp.where(row == 0, p2, jnp.where(row == 1, p1, pltpu.roll(u, 2, 0)))
        carry_ref[c] = u[tm - SUBLANES:tm]
        last_ref[...] = u[tm - 2:tm]
    cw = cw_ref[...]
    uc = cb_ref[...] + s2 * cw[0:1] + s1 * cw[1:2] + u * cw[2:3]
    act = 0.5 * uc * (1.0 + lax.erf(uc * (2.0 ** -0.5))) * z
    acc_ref[...] += jnp.dot(act.astype(BF16), wd_ref[...], preferred_element_type=F32)

    @pl.when(c == pl.num_programs(1) - 1)
    def _():
        o_ref[...] = x_ref[...] + _rms(acc_ref[...], NORM_EPS) * g2_ref[...]


def _ffn(x, conv_prev, lp, *, bsz, t, tm, tf):
    n, d = x.shape
    dff = lp["w_down"].shape[0]
    kw = lp["conv_w"].shape[0]
    assert kw == 3 and dff % tf == 0 and n % tm == 0 and t >= 2
    nc = dff // tf
    short = t < tm
    in_specs = [
        pl.BlockSpec((tm, d), lambda i, c: (i, 0)),
        pl.BlockSpec((1, d), lambda i, c: (0, 0)),
        pl.BlockSpec((d, tf), lambda i, c: (0, c)),
        pl.BlockSpec((d, tf), lambda i, c: (0, nc + c)),
        pl.BlockSpec((kw, tf), lambda i, c: (0, c)),
        pl.BlockSpec((1, tf), lambda i, c: (0, c)),
        pl.BlockSpec((tf, d), lambda i, c: (c, 0)),
        pl.BlockSpec((1, d), lambda i, c: (0, 0)),
    ]
    args = [x, lp["norm_ffn_pre"].reshape(1, d), lp["w_up"], lp["w_up"], lp["conv_w"],
            lp["conv_b"].reshape(1, dff), lp["w_down"], lp["norm_ffn_post"].reshape(1, d)]
    x_spec = pl.BlockSpec((tm, d), lambda i, c: (i, 0))
    scratch = [pltpu.VMEM((tm, d), BF16), pltpu.VMEM((tm, d), F32)]
    if short:
        assert tm % t == 0
        pad = jnp.concatenate([conv_prev, jnp.zeros((bsz, t - 2, dff), F32)], axis=1).reshape(n, dff)
        in_specs.append(pl.BlockSpec((tm, tf), lambda i, c: (i, c)))
        args.append(pad)
        out_specs = [x_spec, pl.BlockSpec((tm, tf), lambda i, c: (i, c))]
        out_shape = [jax.ShapeDtypeStruct((n, d), F32), jax.ShapeDtypeStruct((n, dff), F32)]
    else:
        assert t % tm == 0
        tiles = t // tm
        in_specs.append(pl.BlockSpec((None, 2, tf), lambda i, c: (i // tiles, 0, c)))
        args.append(conv_prev)
        out_specs = [x_spec, pl.BlockSpec((None, 2, tf), lambda i, c: (i, 0, c))]
        out_shape = [jax.ShapeDtypeStruct((n, d), F32), jax.ShapeDtypeStruct((n // tm, 2, dff), F32)]
        scratch.append(pltpu.VMEM((nc, SUBLANES, tf), F32))
    x_new, aux = pl.pallas_call(
        functools.partial(_ffn_kernel, seq_len=t, short=short),
        grid=(n // tm, nc),
        in_specs=in_specs,
        out_specs=out_specs,
        out_shape=out_shape,
        scratch_shapes=scratch,
        compiler_params=_params("arbitrary", "arbitrary"),
        name="ffn",
    )(*args)
    if short:
        conv_new = aux.reshape(bsz, t, dff)[:, t - 2:]
    else:
        conv_new = aux.reshape(bsz, t // tm, 2, dff)[:, -1]
    return x_new, conv_new


def _rope_tables(pos, rows, width):
    inv = jnp.exp(-math.log(ROPE_THETA) * jnp.arange(HALF_ROT, dtype=F32) * (2.0 / ROT_DIM))
    ang = pos.astype(F32)[:, None] * inv[None, :]
    cos, sin = jnp.cos(ang), jnp.sin(ang)
    t = pos.shape[0]
    zeros = jnp.zeros((t, HALF_ROT), F32)
    rest = DA_SUB - ROT_DIM
    cos_t = jnp.concatenate([cos, cos, jnp.ones((t, rest), F32)], axis=1)
    sup_t = jnp.concatenate([-sin, zeros, jnp.zeros((t, rest), F32)], axis=1)
    sdn_t = jnp.concatenate([zeros, sin, jnp.zeros((t, rest), F32)], axis=1)
    return tuple(jnp.tile(tab, (rows // t, width // DA_SUB)) for tab in (cos_t, sup_t, sdn_t))


def _tiles(n, t):
    tm = next(c for c in (1024, 512, 256, 128) if n % c == 0 and (t % c == 0 or c % t == 0))
    return tm, min(tm, 512), 256


def _col_tile(m, cap):
    return max(c for c in range(LANES, cap + 1, LANES) if m % c == 0)


def _rwkv_blocking(bsz, t):
    chunk = min(t, 64)
    nb = 2 if chunk == 64 else 8
    return chunk, math.gcd(nb, bsz)


def _group_layer(x, bsz, t, lp, state, rope, consts, attention):
    shift_prev, wkv_prev, conv_prev = state
    tm, tm_merge, tf = _tiles(x.shape[0], t)
    chunk, nb = _rwkv_blocking(bsz, t)
    g = lp["norm_mix_pre"]
    rw = _norm_matmul(x, g, lp["w_rw"], tm=tm, tn=_col_tile(lp["w_rw"].shape[1], 1024))
    q, k, v = _qkv(x, g, lp["w_qkv"], rope, tm=tm)
    gates = _norm_matmul(x, g, lp["w_g"], tm=tm, tn=_col_tile(lp["w_g"].shape[1], 1024), epilogue="sigmoid")
    sw = rw.shape[1]
    o_a, wkv_new = _rwkv(rw.reshape(bsz, t, sw), shift_prev, wkv_prev, lp, consts, chunk=chunk, nb=nb)
    dw = v.shape[1]
    o_b = attention(q, k, v)
    x = _merge(o_a.reshape(bsz * t, -1), o_b, gates, x, lp, tm=tm_merge)
    x, conv_new = _ffn(x, conv_prev, lp, bsz=bsz, t=t, tm=tm, tf=tf)
    k_new = k.reshape(bsz, t, dw // (2 * DA_SUB), 2, DA_SUB)
    v_new = v.reshape(bsz, t, dw // (2 * DA_SUB), 2 * DA_SUB)
    shift_new = rw.reshape(bsz, t, sw)[:, t - 1]
    return x, (k_new, v_new, wkv_new, shift_new, conv_new)


def kernel(x_prompt, x_sample, cache_k, cache_v, state_wkv, state_shift, state_conv, page_table, norm_mix_pre, norm_mix_post, norm_ffn_pre, norm_ffn_post, w_in, rw_mu, rw_w0, rw_w_lora, rw_a0, rw_a_lora, rw_g_lora, rw_k_k, rw_k_a, rw_r_k, rw_lnx_w, rw_lnx_b, da_lam_q1, da_lam_k1, da_lam_q2, da_lam_k2, da_subln, w_pa, w_pb, w_o, w_up, conv_w, conv_b, w_down):
    bp, tp, d = x_prompt.shape
    bs, ts, _ = x_sample.shape
    depth = w_in.shape[0]
    rww = rw_w0.shape[1]
    sw = rw_mu.shape[1]
    wl, al = rw_w_lora.shape[1], rw_a_lora.shape[1]
    dw = w_pb.shape[1]
    dff = w_down.shape[1]
    n_pages, page = page_table.shape[1], cache_k.shape[2]
    past = n_pages * page
    assert wl + al == LANES and sw == 3 * rww + wl + al + rw_g_lora.shape[1] and rww % PAIR == 0
    assert dw % LANES == 0 and w_in.shape[2] == sw + 3 * dw + 2 * d

    rope_p = _rope_tables(jnp.arange(tp, dtype=jnp.int32), max(tp, _tiles(bp * tp, tp)[0]), dw)
    rope_s = _rope_tables(past + jnp.arange(ts, dtype=jnp.int32), max(ts, _tiles(bs * ts, ts)[0]), dw)
    consts_p = _rwkv_consts(*_rwkv_blocking(bp, tp), rww)
    consts_s = _rwkv_consts(*_rwkv_blocking(bs, ts), rww)
    cache_kt = jnp.transpose(cache_k, (0, 1, 3, 4, 5, 2)).reshape(depth, cache_k.shape[1], dw, page)
    cache_vr = cache_v.reshape(depth, cache_v.shape[1], page * (dw // LANES), LANES)
    sub_sel = (jnp.arange(dw // DA_SUB)[:, None] == jnp.arange(dw)[None, :] // DA_SUB).astype(F32)

    xp = x_prompt.reshape(bp * tp, d)
    xs = x_sample.reshape(bs * ts, d)
    state_p = (jnp.zeros((bp, sw), F32), jnp.zeros((bp, rww // RW_HEAD, RW_HEAD, RW_HEAD), F32),
               jnp.zeros((bp, 2, dff), F32))
    outs_p, outs_s = [], []
    for l in range(depth):
        wi = w_in[l].astype(BF16)
        lp = {
            "norm_mix_pre": norm_mix_pre[l], "norm_mix_post": norm_mix_post[l],
            "norm_ffn_pre": norm_ffn_pre[l], "norm_ffn_post": norm_ffn_post[l],
            "w_rw": wi[:, :sw], "w_qkv": wi[:, sw:sw + 3 * dw],
            "w_g": wi[:, sw + 3 * dw:],
            "rw_mu": rw_mu[l], "rw_w0": rw_w0[l], "rw_a0": rw_a0[l],
            "wlora_pad": jnp.concatenate([rw_w_lora[l], jnp.zeros((al, rww), F32)], axis=0).astype(BF16),
            "alora_pad": jnp.concatenate([jnp.zeros((wl, rww), F32), rw_a_lora[l]], axis=0).astype(BF16),
            "rw_g_lora": rw_g_lora[l].astype(BF16),
            "rw_k_k": rw_k_k[l], "rw_k_a": rw_k_a[l], "rw_r_k": rw_r_k[l].reshape(-1),
            "rw_lnx_w": rw_lnx_w[l], "rw_lnx_b": rw_lnx_b[l],
            "w_pa": w_pa[l].astype(BF16), "w_pb": w_pb[l].astype(BF16), "w_o": w_o[l].astype(BF16),
            "w_up": w_up[l].astype(BF16), "conv_w": conv_w[l], "conv_b": conv_b[l],
            "w_down": w_down[l].astype(BF16),
        }
        lam_init = 0.8 - 0.6 * math.exp(-0.3 * l)
        lamv = jnp.stack([da_lam_q1[l], da_lam_k1[l], da_lam_q2[l], da_lam_k2[l]])
        subln = da_subln[l]

        def attn_prompt(q, k, v):
            return _flash_attention(q, k, v, lamv, subln, bsz=bp, t=tp, lam_init=lam_init, tile=min(tp, 512), hb=4)

        def attn_sample(q, k, v):
            wq = (q.reshape(bs, 1, ts, dw) * sub_sel[None, :, None, :]).reshape(bs, -1, dw)
            o = _paged_attention(wq, k.reshape(bs, ts, dw), v.reshape(bs, ts, dw), cache_kt, cache_vr,
                                 page_table, lamv, subln, layer=l, lam_init=lam_init)
            return o.reshape(bs * ts, dw)

        xp, out_p = _group_layer(xp, bp, tp, lp, state_p, rope_p, consts_p, attn_prompt)
        xs, out_s = _group_layer(xs, bs, ts, lp, (state_shift[l], state_wkv[l], state_conv[l]), rope_s, consts_s,
                                 attn_sample)
        outs_p.append(out_p)
        outs_s.append(out_s)

    stack = lambda outs, j: jnp.stack([o[j] for o in outs])
    return ((xp.reshape(bp, tp, d), xs.reshape(bs, ts, d))
            + tuple(stack(outs_p, j) for j in range(5)) + tuple(stack(outs_s, j) for j in range(5)))
```

```python
import functools
import math

import jax
import jax.numpy as jnp
from jax import lax
from jax.experimental import pallas as pl
from jax.experimental.pallas import tpu as pltpu

F32 = jnp.float32
BF16 = jnp.bfloat16

RW_HEAD = 64
DA_SUB = 64
ROT_DIM = 16
ROPE_THETA = 500000.0
NORM_EPS = 1e-6
LNX_EPS = 64e-5
SUBLN_EPS = 1e-5
KK_EPS = 1e-12

LANES = 128
SUBLANES = 8
VMEM_LIMIT_BYTES = 48 * 1024 * 1024

PAIR = 2 * RW_HEAD
HALF_ROT = ROT_DIM // 2
TN_DIMS = (((0,), (0,)), ((), ()))
NT_DIMS = (((1,), (1,)), ((), ()))


def _params(*semantics):
    return pltpu.CompilerParams(dimension_semantics=semantics, vmem_limit_bytes=VMEM_LIMIT_BYTES)


def _sigmoid(x):
    return 1.0 / (1.0 + jnp.exp(-x))


def _rms(x, eps):
    return x * lax.rsqrt(jnp.mean(x * x, axis=-1, keepdims=True) + eps)


def _dot(a, b):
    return jnp.dot(a.astype(BF16), b.astype(BF16), preferred_element_type=F32)


def _dot_nt(a, b):
    return lax.dot_general(a.astype(BF16), b.astype(BF16), NT_DIMS, preferred_element_type=F32)


def _dot_tn(a, b):
    return lax.dot_general(a.astype(BF16), b.astype(BF16), TN_DIMS, preferred_element_type=F32)


def _split(x, parts):
    pieces = []
    rest = x
    for _ in range(parts):
        piece = rest.astype(BF16)
        pieces.append(piece)
        rest = rest - piece.astype(F32)
    return pieces


def _sel_dot(sel, x, parts):
    out = None
    for piece in _split(x, parts):
        term = jnp.dot(sel, piece, preferred_element_type=F32)
        out = term if out is None else out + term
    return out


def _dot_sel(x, sel, parts):
    out = None
    for piece in _split(x, parts):
        term = jnp.dot(piece, sel, preferred_element_type=F32)
        out = term if out is None else out + term
    return out


def _rw_gates_kernel(x_ref, g_ref, w_ref, rw_ref, gates_ref, *, tn):
    h = (_rms(x_ref[...], NORM_EPS) * g_ref[...]).astype(BF16)
    sw = rw_ref.shape[1]
    for lo in range(0, sw, tn):
        rw_ref[:, lo:lo + tn] = jnp.dot(h, w_ref[:, lo:lo + tn], preferred_element_type=F32)
    for lo in range(0, gates_ref.shape[1], tn):
        gates_ref[:, lo:lo + tn] = _sigmoid(jnp.dot(h, w_ref[:, sw + lo:sw + lo + tn], preferred_element_type=F32))


def _rw_gates(x, g, w, sw, *, tm):
    n, d = x.shape
    gw = w.shape[1] - sw
    tn = math.gcd(sw, gw)
    assert n % tm == 0 and tn % LANES == 0
    return pl.pallas_call(
        functools.partial(_rw_gates_kernel, tn=tn),
        grid=(n // tm,),
        in_specs=[
            pl.BlockSpec((tm, d), lambda i: (i, 0)),
            pl.BlockSpec((1, d), lambda i: (0, 0)),
            pl.BlockSpec((d, sw + gw), lambda i: (0, 0)),
        ],
        out_specs=[pl.BlockSpec((tm, sw), lambda i: (i, 0)), pl.BlockSpec((tm, gw), lambda i: (i, 0))],
        out_shape=[jax.ShapeDtypeStruct((n, sw), F32), jax.ShapeDtypeStruct((n, gw), F32)],
        compiler_params=_params("parallel"),
        name="rw_gates",
    )(x, g.reshape(1, d), w)


def _qkv_kernel(x_ref, g_ref, w_ref, cos_ref, sup_ref, sdn_ref, q_ref, k_ref, v_ref):
    dw = q_ref.shape[1]
    h = (_rms(x_ref[...], NORM_EPS) * g_ref[...]).astype(BF16)

    def rope(t):
        return (t * cos_ref[...] + pltpu.roll(t, HALF_ROT, 1) * sdn_ref[...]
                + pltpu.roll(t, dw - HALF_ROT, 1) * sup_ref[...])

    q = jnp.dot(h, w_ref[:, 0:dw], preferred_element_type=F32)
    k = jnp.dot(h, w_ref[:, dw:2 * dw], preferred_element_type=F32)
    v_ref[...] = jnp.dot(h, w_ref[:, 2 * dw:3 * dw], preferred_element_type=F32)
    q_ref[...] = rope(q) * (DA_SUB ** -0.5)
    k_ref[...] = rope(k)


def _qkv(x, g, w, rope, *, tm):
    n, d = x.shape
    dw = w.shape[1] // 3
    tab_blocks = rope[0].shape[0] // tm
    assert n % tm == 0 and rope[0].shape == (tab_blocks * tm, dw)
    row = pl.BlockSpec((tm, dw), lambda i: (i, 0))
    return pl.pallas_call(
        _qkv_kernel,
        grid=(n // tm,),
        in_specs=[
            pl.BlockSpec((tm, d), lambda i: (i, 0)),
            pl.BlockSpec((1, d), lambda i: (0, 0)),
            pl.BlockSpec((d, 3 * dw), lambda i: (0, 0)),
        ] + [pl.BlockSpec((tm, dw), lambda i: (i % tab_blocks, 0))] * 3,
        out_specs=[row, row, row],
        out_shape=[jax.ShapeDtypeStruct((n, dw), F32)] * 3,
        compiler_params=_params("parallel"),
        name="qkv_rope",
    )(x, g.reshape(1, d), w, *rope)


def _rwkv_kernel(rw_ref, shift_ref, wkv_ref, mu_ref, w0_ref, wlora_ref, a0_ref, alora_ref, glora_ref,
                 kk_ref, ka_ref, rk_ref, lnw_ref, lnb_ref, tri_ref, ones_ref, dup_ref, coll_ref,
                 o_ref, wkv_out_ref, s_ref, prev_ref, y_ref, *, lora_in):
    i = pl.program_id(1)
    nb, c, sw = rw_ref.shape
    w = w0_ref.shape[1]
    n_pairs = w // PAIR
    c2 = 2 * c
    chains = [(s, p) for s in range(nb) for p in range(n_pairs)]
    n = len(chains)

    @pl.when(i == 0)
    def _():
        prev_ref[...] = shift_ref[...]
        blk = (lax.broadcasted_iota(jnp.int32, (PAIR, PAIR), 0) // RW_HEAD
               == lax.broadcasted_iota(jnp.int32, (PAIR, PAIR), 1) // RW_HEAD)
        for s, p in chains:
            s_ref[s, p] = jnp.where(blk, _dot_sel(wkv_ref[s, p], dup_ref[...], 3), 0.0)

    rw = rw_ref[...].reshape(nb * c, sw)
    row = lax.broadcasted_iota(jnp.int32, (nb * c, 1), 0)
    prev = pltpu.roll(rw, 1, 0)
    for s in range(nb):
        prev = jnp.where(row == s * c, prev_ref[s], prev)
        prev_ref[s] = rw[(s + 1) * c - 1:(s + 1) * c, :]
    rs = rw + mu_ref[...] * (prev - rw)
    r = rs[:, 0:w]
    k = rs[:, w:2 * w]
    v = rs[:, 2 * w:3 * w]
    wa = rs[:, 3 * w:3 * w + lora_in]
    gl = rs[:, 3 * w + lora_in:]

    z = -(w0_ref[...] + _dot(jnp.tanh(wa), wlora_ref[...]))
    w_log = -(jnp.maximum(z, 0.0) + jnp.log1p(jnp.exp(-jnp.abs(z)))) - 0.5
    lw = -jnp.exp(w_log)
    a = _sigmoid(a0_ref[...] + _dot(wa, alora_ref[...]))
    g = _dot(_sigmoid(gl), glora_ref[...])

    def head_sum(t):
        return _dot_sel(t, ones_ref[...], 1)

    kk = k * kk_ref[...]
    kk = kk * lax.rsqrt(head_sum(kk * kk) + KK_EPS)
    km = k * (1.0 + (a - 1.0) * ka_ref[...])
    b = kk * a

    cum = _sel_dot(tri_ref[...], lw, 3)
    tots = [cum[(s + 1) * c - 1:(s + 1) * c, :] for s in range(nb)]
    tot = jnp.concatenate([jnp.broadcast_to(t, (c, w)) for t in tots], axis=0)
    kk_t = kk * jnp.exp(cum - lw)
    r_t = r * jnp.exp(cum)
    inv = jnp.exp(-cum)
    k_h = km * inv
    b_h = b * inv
    tail = jnp.exp(tot - cum)
    k_p = km * tail
    b_p = b * tail
    gams = [jnp.exp(t) for t in tots]

    tt = lax.broadcasted_iota(jnp.int32, (c, c2), 0)
    ss = lax.broadcasted_iota(jnp.int32, (c, c2), 1) % c
    strict = ss < tt
    incl = ss <= tt
    eye = (ss == tt).astype(F32)
    keep_lane = (lax.broadcasted_iota(jnp.int32, (c2, PAIR), 0) // c
                 == lax.broadcasted_iota(jnp.int32, (c2, PAIR), 1) // RW_HEAD)
    keep_time = (lax.broadcasted_iota(jnp.int32, (c2, c2), 0) // c
                 == lax.broadcasted_iota(jnp.int32, (c2, c2), 1) // c)
    blk = (lax.broadcasted_iota(jnp.int32, (PAIR, PAIR), 0) // RW_HEAD
           == lax.broadcasted_iota(jnp.int32, (PAIR, PAIR), 1) // RW_HEAD)

    def stack(t, keep):
        return jnp.where(keep, jnp.concatenate([t, t], axis=0), 0.0).astype(BF16)

    def pieces(t):
        return [t[s * c:(s + 1) * c, p * PAIR:(p + 1) * PAIR] for s, p in chains]

    kk_n, r_n, v_n = pieces(kk_t), pieces(r_t), pieces(v)
    kp_n, bp_n = pieces(k_p), pieces(b_p)
    y_k = [stack(t, keep_lane) for t in pieces(k_h)]
    y_b = [stack(t, keep_lane) for t in pieces(b_h)]
    v_s = [stack(t, keep_lane) for t in v_n]
    lhs = [jnp.concatenate([kk_n[j], r_n[j]], axis=0) for j in range(n)]
    g_k = [_dot_nt(lhs[j], y_k[j]) for j in range(n)]
    g_b = [_dot_nt(lhs[j], y_b[j]) for j in range(n)]
    low = [jnp.where(strict, t[:c], 0.0) for t in g_b]
    t_inv = [eye - jnp.where((tt // 2) == (ss // 2), t, 0.0) for t in low]
    m = 2
    while m < c:
        sel = ((tt // (2 * m)) == (ss // (2 * m))) & ((tt // m) != (ss // m))
        half = [_dot(t_inv[j], stack(jnp.where(sel, low[j], 0.0), keep_time)) for j in range(n)]
        t_inv = [t_inv[j] - _dot(half[j], stack(t_inv[j], keep_time)) for j in range(n)]
        m *= 2
    w1 = [_dot(jnp.where(strict, g_k[j][:c], 0.0), v_s[j]) for j in range(n)]
    pq = [_dot(t_inv[j], jnp.concatenate([stack(kk_n[j], keep_lane), stack(w1[j], keep_lane)], axis=1))
          for j in range(n)]
    s0 = [s_ref[s, p] for s, p in chains]
    u = [_dot_nt(pq[j][:, :PAIR], s0[j]) + pq[j][:, PAIR:] for j in range(n)]
    y = [_dot_nt(r_n[j], s0[j]) + _dot(jnp.where(incl, g_k[j][c:], 0.0), v_s[j])
         - _dot(jnp.where(incl, g_b[j][c:], 0.0), stack(u[j], keep_lane)) for j in range(n)]
    for j, (s, p) in enumerate(chains):
        y_ref[s * c:(s + 1) * c, p * PAIR:(p + 1) * PAIR] = y[j]
        vu = jnp.concatenate([v_n[j], u[j]], axis=0)
        kb = jnp.concatenate([kp_n[j], -bp_n[j]], axis=0)
        s_ref[s, p] = s0[j] * gams[s][:, p * PAIR:(p + 1) * PAIR] + jnp.where(blk, _dot_tn(vu, kb), 0.0)

    y = y_ref[...]
    inv_n = 1.0 / RW_HEAD
    d = y - head_sum(y) * inv_n
    yn = d * lax.rsqrt(head_sum(d * d) * inv_n + LNX_EPS) * lnw_ref[...] + lnb_ref[...]
    bonus = head_sum(r * km * rk_ref[...]) * v
    o_ref[...] = ((yn + bonus) * g).reshape(nb, c, w)

    @pl.when(i == pl.num_programs(1) - 1)
    def _():
        for s, p in chains:
            wkv_out_ref[s, p] = _dot_sel(s_ref[s, p], coll_ref[...], 3)


def _rwkv(rw, shift_prev, wkv_prev, lp, consts, *, chunk, nb):
    bsz, t, sw = rw.shape
    w = lp["rw_w0"].shape[-1]
    n_pairs = w // PAIR
    lora_in = lp["wlora_pad"].shape[0]
    assert t % chunk == 0 and bsz % nb == 0 and sw == 3 * w + lora_in + lp["rw_g_lora"].shape[0]
    vec = lambda a: a.reshape(1, -1)
    full = lambda shape: pl.BlockSpec(shape, lambda b, i: (0,) * len(shape))
    tri, ones_blk, dup, coll = consts
    state_spec = pl.BlockSpec((nb, n_pairs, PAIR, RW_HEAD), lambda b, i: (b, 0, 0, 0))
    in_specs = [
        pl.BlockSpec((nb, chunk, sw), lambda b, i: (b, i, 0)),
        pl.BlockSpec((nb, 1, sw), lambda b, i: (b, 0, 0)),
        state_spec,
        full((1, sw)), full((1, w)), full((lora_in, w)), full((1, w)), full((lora_in, w)),
        full(lp["rw_g_lora"].shape), full((1, w)), full((1, w)), full((1, w)), full((1, w)), full((1, w)),
        full(tri.shape), full(ones_blk.shape), full(dup.shape), full(coll.shape),
    ]
    o, wkv_new = pl.pallas_call(
        functools.partial(_rwkv_kernel, lora_in=lora_in),
        grid=(bsz // nb, t // chunk),
        in_specs=in_specs,
        out_specs=[pl.BlockSpec((nb, chunk, w), lambda b, i: (b, i, 0)), state_spec],
        out_shape=[
            jax.ShapeDtypeStruct((bsz, t, w), F32),
            jax.ShapeDtypeStruct((bsz, n_pairs, PAIR, RW_HEAD), F32),
        ],
        scratch_shapes=[
            pltpu.VMEM((nb, n_pairs, PAIR, PAIR), F32),
            pltpu.VMEM((nb, 1, sw), F32),
            pltpu.VMEM((nb * chunk, w), F32),
        ],
        compiler_params=_params("parallel", "arbitrary"),
        name="rwkv",
    )(rw, shift_prev.reshape(bsz, 1, sw), wkv_prev.reshape(bsz, n_pairs, PAIR, RW_HEAD),
      vec(lp["rw_mu"]), vec(lp["rw_w0"]), lp["wlora_pad"], vec(lp["rw_a0"]), lp["alora_pad"],
      lp["rw_g_lora"], vec(lp["rw_k_k"]), vec(lp["rw_k_a"]), vec(lp["rw_r_k"]),
      vec(lp["rw_lnx_w"]), vec(lp["rw_lnx_b"]), tri, ones_blk, dup, coll)
    return o, wkv_new.reshape(bsz, w // RW_HEAD, RW_HEAD, RW_HEAD)


def _rwkv_consts(chunk, nb, w):
    idx = jnp.arange(nb * chunk)
    tri = ((idx[:, None] >= idx[None, :]) & (idx[:, None] // chunk == idx[None, :] // chunk)).astype(BF16)
    hid = jnp.arange(w) // RW_HEAD
    ones_blk = (hid[:, None] == hid[None, :]).astype(BF16)
    dup = (jnp.arange(RW_HEAD)[:, None] == (jnp.arange(PAIR) % RW_HEAD)[None, :]).astype(BF16)
    return tri, ones_blk, dup, dup.T


def _lambda(lamv_ref, lam_init):
    lamv = lamv_ref[...]
    l1 = jnp.sum(lamv[0:1] * lamv[1:2], axis=1, keepdims=True)
    l2 = jnp.sum(lamv[2:3] * lamv[3:4], axis=1, keepdims=True)
    return jnp.exp(l1) - jnp.exp(l2) + lam_init


def _flash_kernel(qt_ref, kt_ref, lamv_ref, g_ref, q_ref, k_ref, v_ref, o_ref, m_ref, a_ref, *, lam_init):
    step = pl.program_id(2)
    qi = qt_ref[step]
    ki = kt_ref[step]
    tq = q_ref.shape[0]
    tk = k_ref.shape[0]
    hb = q_ref.shape[1] // LANES
    n = 2 * hb

    @pl.when(ki == 0)
    def _():
        m_ref[...] = jnp.full(m_ref.shape, -jnp.inf, F32)
        a_ref[...] = jnp.zeros(a_ref.shape, F32)

    def update(masked):
        lane = lax.broadcasted_iota(jnp.int32, (1, LANES), 1)
        qs, ks, vs = [], [], []
        for h in range(hb):
            cs = slice(h * LANES, (h + 1) * LANES)
            q = q_ref[:, cs]
            k = k_ref[:, cs].astype(BF16)
            v = v_ref[:, cs].astype(BF16)
            v_ext = jnp.concatenate([v, jnp.ones_like(v)], axis=1)
            qs += [jnp.where(lane < DA_SUB, q, 0.0).astype(BF16), jnp.where(lane >= DA_SUB, q, 0.0).astype(BF16)]
            ks += [k, k]
            vs += [v_ext, v_ext]
        s = [lax.dot_general(qs[j], ks[j], NT_DIMS, preferred_element_type=F32) for j in range(n)]
        if masked:
            causal = (lax.broadcasted_iota(jnp.int32, (tq, tk), 1) <= lax.broadcasted_iota(jnp.int32, (tq, tk), 0))
            s = [jnp.where(causal, t, -jnp.inf) for t in s]
        m_prev = [m_ref[j] for j in range(n)]
        m_new = [jnp.maximum(m_prev[j], jnp.max(s[j], axis=1, keepdims=True)) for j in range(n)]
        alpha = [jnp.exp(m_prev[j] - m_new[j]) for j in range(n)]
        p = [jnp.exp(s[j] - jnp.tile(m_new[j], (1, tk // LANES))).astype(BF16) for j in range(n)]
        pv = [jnp.dot(p[j], vs[j], preferred_element_type=F32) for j in range(n)]
        for j in range(n):
            a_ref[j] = jnp.concatenate([alpha[j], alpha[j]], axis=1) * a_ref[j] + pv[j]
            m_ref[j] = m_new[j]

    @pl.when(ki < qi)
    def _():
        update(False)

    @pl.when(ki == qi)
    def _():
        update(True)
        lam = _lambda(lamv_ref, lam_init)
        for h in range(hb):
            a1 = a_ref[2 * h]
            a2 = a_ref[2 * h + 1]
            o = a1[:, :LANES] / a1[:, LANES:] - lam * (a2[:, :LANES] / a2[:, LANES:])
            o_ref[:, h * LANES:(h + 1) * LANES] = _rms(o, SUBLN_EPS) * g_ref[...] * (1.0 - lam_init)


def _flash_attention(q, k, v, lamv, subln, *, bsz, t, lam_init, tile, hb):
    n, dw = v.shape
    heads = dw // LANES
    assert t % tile == 0 and n == bsz * t and heads % hb == 0
    nt = t // tile
    groups = heads // hb
    pairs = [(qi, ki) for qi in range(nt) for ki in range(qi + 1)]
    q_tab = jnp.array([p[0] for p in pairs], jnp.int32)
    k_tab = jnp.array([p[1] for p in pairs], jnp.int32)
    width = hb * LANES
    grid_spec = pltpu.PrefetchScalarGridSpec(
        num_scalar_prefetch=2,
        grid=(bsz, groups, len(pairs)),
        in_specs=[
            pl.BlockSpec(lamv.shape, lambda b, h, s, qt, kt: (0, 0)),
            pl.BlockSpec((1, LANES), lambda b, h, s, qt, kt: (0, 0)),
            pl.BlockSpec((tile, width), lambda b, h, s, qt, kt: (b * nt + qt[s], h)),
            pl.BlockSpec((tile, width), lambda b, h, s, qt, kt: (b * nt + kt[s], h)),
            pl.BlockSpec((tile, width), lambda b, h, s, qt, kt: (b * nt + kt[s], h)),
        ],
        out_specs=pl.BlockSpec((tile, width), lambda b, h, s, qt, kt: (b * nt + qt[s], h)),
        scratch_shapes=[pltpu.VMEM((2 * hb, tile, LANES), F32), pltpu.VMEM((2 * hb, tile, 2 * LANES), F32)],
    )
    return pl.pallas_call(
        functools.partial(_flash_kernel, lam_init=lam_init),
        grid_spec=grid_spec,
        out_shape=jax.ShapeDtypeStruct((n, dw), F32),
        compiler_params=_params("parallel", "parallel", "arbitrary"),
        name="flash_diff_attention",
    )(q_tab, k_tab, lamv, subln.reshape(1, LANES), q, k, v)


def _paged_kernel(pt_ref, lamv_ref, g_ref, wq_ref, kn_ref, vn_ref, *rest, n_pages, lam_init):
    kt_refs = rest[:n_pages]
    v_refs = rest[n_pages:2 * n_pages]
    o_ref = rest[2 * n_pages]
    tq = kn_ref.shape[0]
    heads = o_ref.shape[1] // LANES
    page = kt_refs[0].shape[1]
    rows = 2 * tq
    wq = wq_ref[...].astype(BF16)
    s_past = [jnp.dot(wq, kt[...].astype(BF16), preferred_element_type=F32) for kt in kt_refs]
    s_new = lax.dot_general(wq, kn_ref[...].astype(BF16), NT_DIMS, preferred_element_type=F32)
    qry_t = lax.broadcasted_iota(jnp.int32, s_new.shape, 0) % tq
    key_t = lax.broadcasted_iota(jnp.int32, s_new.shape, 1)
    s_new = jnp.where(key_t <= qry_t, s_new, -jnp.inf)
    m_past = s_past[0]
    for s in s_past[1:]:
        m_past = jnp.maximum(m_past, s)
    m = jnp.maximum(jnp.max(m_past, axis=1, keepdims=True), jnp.max(s_new, axis=1, keepdims=True))
    p_past = [jnp.exp(s - m) for s in s_past]
    p_new = jnp.exp(s_new - m)
    p_sum = p_past[0]
    for p in p_past[1:]:
        p_sum = p_sum + p
    den = jnp.sum(p_sum, axis=1, keepdims=True) + jnp.sum(p_new, axis=1, keepdims=True)
    p_past = [p.astype(BF16) for p in p_past]
    p_new = p_new.astype(BF16)
    vn = vn_ref[...].astype(BF16)
    lam = _lambda(lamv_ref, lam_init)
    for h in range(heads):
        rs = slice(h * rows, (h + 1) * rows)
        cs = slice(h * LANES, (h + 1) * LANES)
        acc = jnp.dot(p_new[rs], vn[:, cs], preferred_element_type=F32)
        for p, vr in zip(p_past, v_refs):
            v_h = vr[pl.ds(h, page, stride=heads), :]
            acc = acc + jnp.dot(p[rs], v_h.astype(BF16), preferred_element_type=F32)
        acc = acc / den[rs]
        o = acc[:tq] - lam * acc[tq:]
        o_ref[:, cs] = _rms(o, SUBLN_EPS) * g_ref[...] * (1.0 - lam_init)


def _paged_attention(wq, k_new, v_new, cache_kt, cache_v, page_table, lamv, subln, *, layer, lam_init):
    bsz, tq, dw = k_new.shape
    n_pages = page_table.shape[1]

    def page_spec(shape, j):
        return pl.BlockSpec((None, None) + shape, lambda b, pt: (layer, pt[b * n_pages + j]) + (0,) * len(shape))

    grid_spec = pltpu.PrefetchScalarGridSpec(
        num_scalar_prefetch=1,
        grid=(bsz,),
        in_specs=[
            pl.BlockSpec(lamv.shape, lambda b, pt: (0, 0)),
            pl.BlockSpec((1, LANES), lambda b, pt: (0, 0)),
            pl.BlockSpec((None,) + wq.shape[1:], lambda b, pt: (b, 0, 0)),
            pl.BlockSpec((None, tq, dw), lambda b, pt: (b, 0, 0)),
            pl.BlockSpec((None, tq, dw), lambda b, pt: (b, 0, 0)),
        ] + [page_spec(cache_kt.shape[2:], j) for j in range(n_pages)]
          + [page_spec(cache_v.shape[2:], j) for j in range(n_pages)],
        out_specs=pl.BlockSpec((None, tq, dw), lambda b, pt: (b, 0, 0)),
    )
    return pl.pallas_call(
        functools.partial(_paged_kernel, n_pages=n_pages, lam_init=lam_init),
        grid_spec=grid_spec,
        out_shape=jax.ShapeDtypeStruct((bsz, tq, dw), F32),
        compiler_params=_params("parallel"),
        name="paged_diff_attention",
    )(page_table.reshape(-1), lamv, subln.reshape(1, LANES), wq, k_new, v_new,
      *([cache_kt] * n_pages), *([cache_v] * n_pages))


def _merge_kernel(oa_ref, ob_ref, ga_ref, gb_ref, x_ref, wpa_ref, wpb_ref, wo_ref, g_ref, o_ref):
    m = ga_ref[...] * _dot(oa_ref[...], wpa_ref[...]) + gb_ref[...] * _dot(ob_ref[...], wpb_ref[...])
    a = _dot(m, wo_ref[...])
    o_ref[...] = x_ref[...] + _rms(a, NORM_EPS) * g_ref[...]


def _merge(o_a, o_b, gates, x, lp, *, tm):
    n, d = x.shape
    wd = o_a.shape[1]
    row = lambda width, j=0: pl.BlockSpec((tm, width), lambda i: (i, j))
    full = lambda shape: pl.BlockSpec(shape, lambda i: (0, 0))
    return pl.pallas_call(
        _merge_kernel,
        grid=(n // tm,),
        in_specs=[row(wd), row(wd), row(d, 0), row(d, 1), row(d),
                  full((wd, d)), full((wd, d)), full((d, d)), full((1, d))],
        out_specs=row(d),
        out_shape=jax.ShapeDtypeStruct((n, d), F32),
        compiler_params=_params("parallel"),
        name="merge",
    )(o_a, o_b, gates, gates, x, lp["w_pa"], lp["w_pb"], lp["w_o"], lp["norm_mix_post"].reshape(1, d))


def _ffn_kernel(*refs, seq_len, short):
    if short:
        x_ref, g1_ref, wu_ref, wz_ref, cw_ref, cb_ref, wd_ref, g2_ref, pad_ref, o_ref, u_ref, h_ref, acc_ref = refs
    else:
        (x_ref, g1_ref, wu_ref, wz_ref, cw_ref, cb_ref, wd_ref, g2_ref, cprev_ref,
         o_ref, last_ref, h_ref, acc_ref, carry_ref) = refs
    i = pl.program_id(0)
    c = pl.program_id(1)
    tm = x_ref.shape[0]

    @pl.when(c == 0)
    def _():
        h_ref[...] = (_rms(x_ref[...], NORM_EPS) * g1_ref[...]).astype(BF16)
        acc_ref[...] = jnp.zeros(acc_ref.shape, F32)

    h = h_ref[...]
    u = jnp.dot(h, wu_ref[...], preferred_element_type=F32)
    z = jnp.dot(h, wz_ref[...], preferred_element_type=F32)
    row = lax.broadcasted_iota(jnp.int32, (tm, 1), 0)
    if short:
        pad = pad_ref[...]
        t = row % seq_len
        s1 = jnp.where(t == 0, pltpu.roll(pad, tm - 1, 0), pltpu.roll(u, 1, 0))
        s2 = jnp.where(t < 2, pad, pltpu.roll(u, 2, 0))
        u_ref[...] = u
    else:
        @pl.when(i % (seq_len // tm) == 0)
        def _():
            carry_ref[c, pl.ds(SUBLANES - 2, 2), :] = cprev_ref[...]

        before = carry_ref[c, pl.ds(SUBLANES - 2, 2), :]
        p2 = before[0:1]
        p1 = before[1:2]
        s1 = jnp.where(row == 0, p1, pltpu.roll(u, 1, 0))
        s2 = jnp.where(row == 0, p2, jnp.where(row == 1, p1, pltpu.roll(u, 2, 0)))
        carry_ref[c] = u[tm - SUBLANES:tm]
        last_ref[...] = u[tm - 2:tm]
    cw = cw_ref[...]
    uc = cb_ref[...] + s2 * cw[0:1] + s1 * cw[1:2] + u * cw[2:3]
    act = 0.5 * uc * (1.0 + lax.erf(uc * (2.0 ** -0.5))) * z
    acc_ref[...] += jnp.dot(act.astype(BF16), wd_ref[...], preferred_element_type=F32)

    @pl.when(c == pl.num_programs(1) - 1)
    def _():
        o_ref[...] = x_ref[...] + _rms(acc_ref[...], NORM_EPS) * g2_ref[...]


def _ffn(x, conv_prev, lp, *, bsz, t, tm, tf):
    n, d = x.shape
    dff = lp["w_down"].shape[0]
    kw = lp["conv_w"].shape[0]
    assert kw == 3 and dff % tf == 0 and n % tm == 0 and t >= 2
    nc = dff // tf
    short = t < tm
    in_specs = [
        pl.BlockSpec((tm, d), lambda i, c: (i, 0)),
        pl.BlockSpec((1, d), lambda i, c: (0, 0)),
        pl.BlockSpec((d, tf), lambda i, c: (0, c)),
        pl.BlockSpec((d, tf), lambda i, c: (0, nc + c)),
        pl.BlockSpec((kw, tf), lambda i, c: (0, c)),
        pl.BlockSpec((1, tf), lambda i, c: (0, c)),
        pl.BlockSpec((tf, d), lambda i, c: (c, 0)),
        pl.BlockSpec((1, d), lambda i, c: (0, 0)),
    ]
    args = [x, lp["norm_ffn_pre"].reshape(1, d), lp["w_up"], lp["w_up"], lp["conv_w"],
            lp["conv_b"].reshape(1, dff), lp["w_down"], lp["norm_ffn_post"].reshape(1, d)]
    x_spec = pl.BlockSpec((tm, d), lambda i, c: (i, 0))
    scratch = [pltpu.VMEM((tm, d), BF16), pltpu.VMEM((tm, d), F32)]
    if short:
        assert tm % t == 0
        pad = jnp.concatenate([conv_prev, jnp.zeros((bsz, t - 2, dff), F32)], axis=1).reshape(n, dff)
        in_specs.append(pl.BlockSpec((tm, tf), lambda i, c: (i, c)))
        args.append(pad)
        out_specs = [x_spec, pl.BlockSpec((tm, tf), lambda i, c: (i, c))]
        out_shape = [jax.ShapeDtypeStruct((n, d), F32), jax.ShapeDtypeStruct((n, dff), F32)]
    else:
        assert t % tm == 0
        tiles = t // tm
        in_specs.append(pl.BlockSpec((None, 2, tf), lambda i, c: (i // tiles, 0, c)))
        args.append(conv_prev)
        out_specs = [x_spec, pl.BlockSpec((None, 2, tf), lambda i, c: (i, 0, c))]
        out_shape = [jax.ShapeDtypeStruct((n, d), F32), jax.ShapeDtypeStruct((n // tm, 2, dff), F32)]
        scratch.append(pltpu.VMEM((nc, SUBLANES, tf), F32))
    x_new, aux = pl.pallas_call(
        functools.partial(_ffn_kernel, seq_len=t, short=short),
        grid=(n // tm, nc),
        in_specs=in_specs,
        out_specs=out_specs,
        out_shape=out_shape,
        scratch_shapes=scratch,
        compiler_params=_params("arbitrary", "arbitrary"),
        name="ffn",
    )(*args)
    if short:
        conv_new = aux.reshape(bsz, t, dff)[:, t - 2:]
    else:
        conv_new = aux.reshape(bsz, t // tm, 2, dff)[:, -1]
    return x_new, conv_new


def _rope_tables(pos, rows, width):
    inv = jnp.exp(-math.log(ROPE_THETA) * jnp.arange(HALF_ROT, dtype=F32) * (2.0 / ROT_DIM))
    ang = pos.astype(F32)[:, None] * inv[None, :]
    cos, sin = jnp.cos(ang), jnp.sin(ang)
    t = pos.shape[0]
    zeros = jnp.zeros((t, HALF_ROT), F32)
    rest = DA_SUB - ROT_DIM
    cos_t = jnp.concatenate([cos, cos, jnp.ones((t, rest), F32)], axis=1)
    sup_t = jnp.concatenate([-sin, zeros, jnp.zeros((t, rest), F32)], axis=1)
    sdn_t = jnp.concatenate([zeros, sin, jnp.zeros((t, rest), F32)], axis=1)
    return tuple(jnp.tile(tab, (rows // t, width // DA_SUB)) for tab in (cos_t, sup_t, sdn_t))


def _tiles(n, t):
    tm = next(c for c in (1024, 512, 256, 128) if n % c == 0 and (t % c == 0 or c % t == 0))
    return tm, min(tm, 512), 256


def _rwkv_blocking(bsz, t):
    chunk = min(t, 64)
    nb = 2 if chunk == 64 else 8
    return chunk, math.gcd(nb, bsz)


def _group_layer(x, bsz, t, lp, state, rope, consts, attention):
    shift_prev, wkv_prev, conv_prev = state
    tm, tm_merge, tf = _tiles(x.shape[0], t)
    chunk, nb = _rwkv_blocking(bsz, t)
    g = lp["norm_mix_pre"]
    sw = lp["rw_mu"].shape[0]
    rw, gates = _rw_gates(x, g, lp["w_rwg"], sw, tm=tm_merge)
    q, k, v = _qkv(x, g, lp["w_qkv"], rope, tm=tm)
    o_a, wkv_new = _rwkv(rw.reshape(bsz, t, sw), shift_prev, wkv_prev, lp, consts, chunk=chunk, nb=nb)
    dw = v.shape[1]
    o_b = attention(q, k, v)
    x = _merge(o_a.reshape(bsz * t, -1), o_b, gates, x, lp, tm=tm_merge)
    x, conv_new = _ffn(x, conv_prev, lp, bsz=bsz, t=t, tm=tm, tf=tf)
    k_new = k.reshape(bsz, t, dw // (2 * DA_SUB), 2, DA_SUB)
    v_new = v.reshape(bsz, t, dw // (2 * DA_SUB), 2 * DA_SUB)
    shift_new = rw.reshape(bsz, t, sw)[:, t - 1]
    return x, (k_new, v_new, wkv_new, shift_new, conv_new)


def kernel(x_prompt, x_sample, cache_k, cache_v, state_wkv, state_shift, state_conv, page_table, norm_mix_pre, norm_mix_post, norm_ffn_pre, norm_ffn_post, w_in, rw_mu, rw_w0, rw_w_lora, rw_a0, rw_a_lora, rw_g_lora, rw_k_k, rw_k_a, rw_r_k, rw_lnx_w, rw_lnx_b, da_lam_q1, da_lam_k1, da_lam_q2, da_lam_k2, da_subln, w_pa, w_pb, w_o, w_up, conv_w, conv_b, w_down):
    bp, tp, d = x_prompt.shape
    bs, ts, _ = x_sample.shape
    depth = w_in.shape[0]
    rww = rw_w0.shape[1]
    sw = rw_mu.shape[1]
    wl, al = rw_w_lora.shape[1], rw_a_lora.shape[1]
    dw = w_pb.shape[1]
    dff = w_down.shape[1]
    n_pages, page = page_table.shape[1], cache_k.shape[2]
    past = n_pages * page
    assert wl + al == LANES and sw == 3 * rww + wl + al + rw_g_lora.shape[1] and rww % PAIR == 0
    assert dw % LANES == 0 and w_in.shape[2] == sw + 3 * dw + 2 * d

    rope_p = _rope_tables(jnp.arange(tp, dtype=jnp.int32), max(tp, _tiles(bp * tp, tp)[0]), dw)
    rope_s = _rope_tables(past + jnp.arange(ts, dtype=jnp.int32), max(ts, _tiles(bs * ts, ts)[0]), dw)
    consts_p = _rwkv_consts(*_rwkv_blocking(bp, tp), rww)
    consts_s = _rwkv_consts(*_rwkv_blocking(bs, ts), rww)
    cache_kt = jnp.transpose(cache_k, (0, 1, 3, 4, 5, 2)).reshape(depth, cache_k.shape[1], dw, page)
    cache_vr = cache_v.reshape(depth, cache_v.shape[1], page * (dw // LANES), LANES)
    sub_sel = (jnp.arange(dw // DA_SUB)[:, None] == jnp.arange(dw)[None, :] // DA_SUB).astype(F32)

    xp = x_prompt.reshape(bp * tp, d)
    xs = x_sample.reshape(bs * ts, d)
    state_p = (jnp.zeros((bp, sw), F32), jnp.zeros((bp, rww // RW_HEAD, RW_HEAD, RW_HEAD), F32),
               jnp.zeros((bp, 2, dff), F32))
    outs_p, outs_s = [], []
    for l in range(depth):
        wi = w_in[l].astype(BF16)
        lp = {
            "norm_mix_pre": norm_mix_pre[l], "norm_mix_post": norm_mix_post[l],
            "norm_ffn_pre": norm_ffn_pre[l], "norm_ffn_post": norm_ffn_post[l],
            "w_rwg": jnp.concatenate([wi[:, :sw], wi[:, sw + 3 * dw:]], axis=1), "w_qkv": wi[:, sw:sw + 3 * dw],
            "rw_mu": rw_mu[l], "rw_w0": rw_w0[l], "rw_a0": rw_a0[l],
            "wlora_pad": jnp.concatenate([rw_w_lora[l], jnp.zeros((al, rww), F32)], axis=0).astype(BF16),
            "alora_pad": jnp.concatenate([jnp.zeros((wl, rww), F32), rw_a_lora[l]], axis=0).astype(BF16),
            "rw_g_lora": rw_g_lora[l].astype(BF16),
            "rw_k_k": rw_k_k[l], "rw_k_a": rw_k_a[l], "rw_r_k": rw_r_k[l].reshape(-1),
            "rw_lnx_w": rw_lnx_w[l], "rw_lnx_b": rw_lnx_b[l],
            "w_pa": w_pa[l].astype(BF16), "w_pb": w_pb[l].astype(BF16), "w_o": w_o[l].astype(BF16),
            "w_up": w_up[l].astype(BF16), "conv_w": conv_w[l], "conv_b": conv_b[l],
            "w_down": w_down[l].astype(BF16),
        }
        lam_init = 0.8 - 0.6 * math.exp(-0.3 * l)
        lamv = jnp.stack([da_lam_q1[l], da_lam_k1[l], da_lam_q2[l], da_lam_k2[l]])
        subln = da_subln[l]

        def attn_prompt(q, k, v):
            return _flash_attention(q, k, v, lamv, subln, bsz=bp, t=tp, lam_init=lam_init, tile=min(tp, 512), hb=4)

        def attn_sample(q, k, v):
            wq = (q.reshape(bs, 1, ts, dw) * sub_sel[None, :, None, :]).reshape(bs, -1, dw)
            o = _paged_attention(wq, k.reshape(bs, ts, dw), v.reshape(bs, ts, dw), cache_kt, cache_vr,
                                 page_table, lamv, subln, layer=l, lam_init=lam_init)
            return o.reshape(bs * ts, dw)

        xp, out_p = _group_layer(xp, bp, tp, lp, state_p, rope_p, consts_p, attn_prompt)
        xs, out_s = _group_layer(xs, bs, ts, lp, (state_shift[l], state_wkv[l], state_conv[l]), rope_s, consts_s,
                                 attn_sample)
        outs_p.append(out_p)
        outs_s.append(out_s)

    stack = lambda outs, j: jnp.stack([o[j] for o in outs])
    return ((xp.reshape(bp, tp, d), xs.reshape(bs, ts, d))
            + tuple(stack(outs_p, j) for j in range(5)) + tuple(stack(outs_s, j) for j in range(5)))
```

```python
import functools
import math

import jax
import jax.numpy as jnp
from jax import lax
from jax.experimental import pallas as pl
from jax.experimental.pallas import tpu as pltpu

F32 = jnp.float32
BF16 = jnp.bfloat16

RW_HEAD = 64
DA_SUB = 64
ROT_DIM = 16
ROPE_THETA = 500000.0
NORM_EPS = 1e-6
LNX_EPS = 64e-5
SUBLN_EPS = 1e-5
KK_EPS = 1e-12

LANES = 128
SUBLANES = 8
VMEM_LIMIT_BYTES = 48 * 1024 * 1024

FFN_ROW_BLOCKS = 2
PAIR = 2 * RW_HEAD
HALF_ROT = ROT_DIM // 2
TN_DIMS = (((0,), (0,)), ((), ()))
NT_DIMS = (((1,), (1,)), ((), ()))


def _params(*semantics):
    return pltpu.CompilerParams(dimension_semantics=semantics, vmem_limit_bytes=VMEM_LIMIT_BYTES)


def _sigmoid(x):
    return 1.0 / (1.0 + jnp.exp(-x))


def _rms(x, eps):
    return x * lax.rsqrt(jnp.mean(x * x, axis=-1, keepdims=True) + eps)


def _dot(a, b):
    return jnp.dot(a.astype(BF16), b.astype(BF16), preferred_element_type=F32)


def _dot_nt(a, b):
    return lax.dot_general(a.astype(BF16), b.astype(BF16), NT_DIMS, preferred_element_type=F32)


def _dot_tn(a, b):
    return lax.dot_general(a.astype(BF16), b.astype(BF16), TN_DIMS, preferred_element_type=F32)


def _split(x, parts):
    pieces = []
    rest = x
    for _ in range(parts):
        piece = rest.astype(BF16)
        pieces.append(piece)
        rest = rest - piece.astype(F32)
    return pieces


def _sel_dot(sel, x, parts):
    out = None
    for piece in _split(x, parts):
        term = jnp.dot(sel, piece, preferred_element_type=F32)
        out = term if out is None else out + term
    return out


def _dot_sel(x, sel, parts):
    out = None
    for piece in _split(x, parts):
        term = jnp.dot(piece, sel, preferred_element_type=F32)
        out = term if out is None else out + term
    return out


def _rw_gates_kernel(x_ref, g_ref, w_ref, rw_ref, gates_ref, *, tn):
    h = (_rms(x_ref[...], NORM_EPS) * g_ref[...]).astype(BF16)
    sw = rw_ref.shape[1]
    for lo in range(0, sw, tn):
        rw_ref[:, lo:lo + tn] = jnp.dot(h, w_ref[:, lo:lo + tn], preferred_element_type=F32)
    for lo in range(0, gates_ref.shape[1], tn):
        gates_ref[:, lo:lo + tn] = _sigmoid(jnp.dot(h, w_ref[:, sw + lo:sw + lo + tn], preferred_element_type=F32))


def _rw_gates(x, g, w, sw, *, tm):
    n, d = x.shape
    gw = w.shape[1] - sw
    tn = math.gcd(sw, gw)
    assert n % tm == 0 and tn % LANES == 0
    return pl.pallas_call(
        functools.partial(_rw_gates_kernel, tn=tn),
        grid=(n // tm,),
        in_specs=[
            pl.BlockSpec((tm, d), lambda i: (i, 0)),
            pl.BlockSpec((1, d), lambda i: (0, 0)),
            pl.BlockSpec((d, sw + gw), lambda i: (0, 0)),
        ],
        out_specs=[pl.BlockSpec((tm, sw), lambda i: (i, 0)), pl.BlockSpec((tm, gw), lambda i: (i, 0))],
        out_shape=[jax.ShapeDtypeStruct((n, sw), F32), jax.ShapeDtypeStruct((n, gw), F32)],
        compiler_params=_params("parallel"),
        name="rw_gates",
    )(x, g.reshape(1, d), w)


def _qkv_kernel(x_ref, g_ref, w_ref, cos_ref, sup_ref, sdn_ref, q_ref, k_ref, v_ref):
    dw = q_ref.shape[1]
    h = (_rms(x_ref[...], NORM_EPS) * g_ref[...]).astype(BF16)

    def rope(t):
        return (t * cos_ref[...] + pltpu.roll(t, HALF_ROT, 1) * sdn_ref[...]
                + pltpu.roll(t, dw - HALF_ROT, 1) * sup_ref[...])

    q = jnp.dot(h, w_ref[:, 0:dw], preferred_element_type=F32)
    k = jnp.dot(h, w_ref[:, dw:2 * dw], preferred_element_type=F32)
    v_ref[...] = jnp.dot(h, w_ref[:, 2 * dw:3 * dw], preferred_element_type=F32)
    q_ref[...] = rope(q) * (DA_SUB ** -0.5)
    k_ref[...] = rope(k)


def _qkv(x, g, w, rope, *, tm):
    n, d = x.shape
    dw = w.shape[1] // 3
    tab_blocks = rope[0].shape[0] // tm
    assert n % tm == 0 and rope[0].shape == (tab_blocks * tm, dw)
    row = pl.BlockSpec((tm, dw), lambda i: (i, 0))
    return pl.pallas_call(
        _qkv_kernel,
        grid=(n // tm,),
        in_specs=[
            pl.BlockSpec((tm, d), lambda i: (i, 0)),
            pl.BlockSpec((1, d), lambda i: (0, 0)),
            pl.BlockSpec((d, 3 * dw), lambda i: (0, 0)),
        ] + [pl.BlockSpec((tm, dw), lambda i: (i % tab_blocks, 0))] * 3,
        out_specs=[row, row, row],
        out_shape=[jax.ShapeDtypeStruct((n, dw), F32)] * 3,
        compiler_params=_params("parallel"),
        name="qkv_rope",
    )(x, g.reshape(1, d), w, *rope)


def _rwkv_kernel(rw_ref, shift_ref, wkv_ref, mu_ref, w0_ref, wlora_ref, a0_ref, alora_ref, glora_ref,
                 kk_ref, ka_ref, rk_ref, lnw_ref, lnb_ref, tri_ref, ones_ref, dup_ref, coll_ref,
                 o_ref, wkv_out_ref, s_ref, prev_ref, y_ref, *, lora_in, chunk):
    i = pl.program_id(1)
    nb, span, sw = rw_ref.shape
    c = chunk
    cps = span // c
    w = w0_ref.shape[1]
    n_pairs = w // PAIR
    c2 = 2 * c
    chains = [(s, p) for s in range(nb * cps) for p in range(n_pairs)]
    n = len(chains)

    @pl.when(i == 0)
    def _():
        prev_ref[...] = shift_ref[...]
        blk = (lax.broadcasted_iota(jnp.int32, (PAIR, PAIR), 0) // RW_HEAD
               == lax.broadcasted_iota(jnp.int32, (PAIR, PAIR), 1) // RW_HEAD)
        for s in range(nb):
            for p in range(n_pairs):
                s_ref[s, p] = jnp.where(blk, _dot_sel(wkv_ref[s, p], dup_ref[...], 3), 0.0)

    rw = rw_ref[...].reshape(nb * span, sw)
    row = lax.broadcasted_iota(jnp.int32, (nb * span, 1), 0)
    prev = pltpu.roll(rw, 1, 0)
    for s in range(nb):
        prev = jnp.where(row == s * span, prev_ref[s], prev)
        prev_ref[s] = rw[(s + 1) * span - 1:(s + 1) * span, :]
    rs = rw + mu_ref[...] * (prev - rw)
    r = rs[:, 0:w]
    k = rs[:, w:2 * w]
    v = rs[:, 2 * w:3 * w]
    wa = rs[:, 3 * w:3 * w + lora_in]
    gl = rs[:, 3 * w + lora_in:]

    z = -(w0_ref[...] + _dot(jnp.tanh(wa), wlora_ref[...]))
    w_log = -(jnp.maximum(z, 0.0) + jnp.log1p(jnp.exp(-jnp.abs(z)))) - 0.5
    lw = -jnp.exp(w_log)
    a = _sigmoid(a0_ref[...] + _dot(wa, alora_ref[...]))
    g = _dot(_sigmoid(gl), glora_ref[...])

    def head_sum(t):
        return _dot_sel(t, ones_ref[...], 1)

    kk = k * kk_ref[...]
    kk = kk * lax.rsqrt(head_sum(kk * kk) + KK_EPS)
    km = k * (1.0 + (a - 1.0) * ka_ref[...])
    b = kk * a

    cum = _sel_dot(tri_ref[...], lw, 3)
    tots = [cum[(s + 1) * c - 1:(s + 1) * c, :] for s in range(nb * cps)]
    tot = jnp.concatenate([jnp.broadcast_to(t, (c, w)) for t in tots], axis=0)
    kk_t = kk * jnp.exp(cum - lw)
    r_t = r * jnp.exp(cum)
    inv = jnp.exp(-cum)
    k_h = km * inv
    b_h = b * inv
    tail = jnp.exp(tot - cum)
    k_p = km * tail
    b_p = b * tail
    gams = [jnp.exp(t) for t in tots]

    tt = lax.broadcasted_iota(jnp.int32, (c, c2), 0)
    ss = lax.broadcasted_iota(jnp.int32, (c, c2), 1) % c
    strict = ss < tt
    incl = ss <= tt
    eye = (ss == tt).astype(F32)
    keep_lane = (lax.broadcasted_iota(jnp.int32, (c2, PAIR), 0) // c
                 == lax.broadcasted_iota(jnp.int32, (c2, PAIR), 1) // RW_HEAD)
    keep_time = (lax.broadcasted_iota(jnp.int32, (c2, c2), 0) // c
                 == lax.broadcasted_iota(jnp.int32, (c2, c2), 1) // c)
    blk = (lax.broadcasted_iota(jnp.int32, (PAIR, PAIR), 0) // RW_HEAD
           == lax.broadcasted_iota(jnp.int32, (PAIR, PAIR), 1) // RW_HEAD)

    def stack(t, keep):
        return jnp.where(keep, jnp.concatenate([t, t], axis=0), 0.0).astype(BF16)

    def pieces(t):
        return [t[s * c:(s + 1) * c, p * PAIR:(p + 1) * PAIR] for s, p in chains]

    kk_n, r_n, v_n = pieces(kk_t), pieces(r_t), pieces(v)
    kp_n, bp_n = pieces(k_p), pieces(b_p)
    y_k = [stack(t, keep_lane) for t in pieces(k_h)]
    y_b = [stack(t, keep_lane) for t in pieces(b_h)]
    v_s = [stack(t, keep_lane) for t in v_n]
    lhs = [jnp.concatenate([kk_n[j], r_n[j]], axis=0) for j in range(n)]
    g_k = [_dot_nt(lhs[j], y_k[j]) for j in range(n)]
    g_b = [_dot_nt(lhs[j], y_b[j]) for j in range(n)]
    low = [jnp.where(strict, t[:c], 0.0) for t in g_b]
    t_inv = [eye - jnp.where((tt // 2) == (ss // 2), t, 0.0) for t in low]
    m = 2
    while m < c:
        sel = ((tt // (2 * m)) == (ss // (2 * m))) & ((tt // m) != (ss // m))
        half = [_dot(t_inv[j], stack(jnp.where(sel, low[j], 0.0), keep_time)) for j in range(n)]
        t_inv = [t_inv[j] - _dot(half[j], stack(t_inv[j], keep_time)) for j in range(n)]
        m *= 2
    w1 = [_dot(jnp.where(strict, g_k[j][:c], 0.0), v_s[j]) for j in range(n)]
    pq = [_dot(t_inv[j], jnp.concatenate([stack(kk_n[j], keep_lane), stack(w1[j], keep_lane)], axis=1))
          for j in range(n)]
    state = {(s, p): s_ref[s, p] for s in range(nb) for p in range(n_pairs)}
    for q in range(cps):
        now = [j for j, (s, p) in enumerate(chains) if s % cps == q]
        s0 = {j: state[(chains[j][0] // cps, chains[j][1])] for j in now}
        u = {j: _dot_nt(pq[j][:, :PAIR], s0[j]) + pq[j][:, PAIR:] for j in now}
        y = {j: _dot_nt(r_n[j], s0[j]) + _dot(jnp.where(incl, g_k[j][c:], 0.0), v_s[j])
             - _dot(jnp.where(incl, g_b[j][c:], 0.0), stack(u[j], keep_lane)) for j in now}
        for j in now:
            s, p = chains[j]
            y_ref[s * c:(s + 1) * c, p * PAIR:(p + 1) * PAIR] = y[j]
            vu = jnp.concatenate([v_n[j], u[j]], axis=0)
            kb = jnp.concatenate([kp_n[j], -bp_n[j]], axis=0)
            state[(s // cps, p)] = (s0[j] * gams[s][:, p * PAIR:(p + 1) * PAIR]
                                    + jnp.where(blk, _dot_tn(vu, kb), 0.0))
    for (s, p), value in state.items():
        s_ref[s, p] = value

    y = y_ref[...]
    inv_n = 1.0 / RW_HEAD
    d = y - head_sum(y) * inv_n
    yn = d * lax.rsqrt(head_sum(d * d) * inv_n + LNX_EPS) * lnw_ref[...] + lnb_ref[...]
    bonus = head_sum(r * km * rk_ref[...]) * v
    o_ref[...] = ((yn + bonus) * g).reshape(nb, span, w)

    @pl.when(i == pl.num_programs(1) - 1)
    def _():
        for s in range(nb):
            for p in range(n_pairs):
                wkv_out_ref[s, p] = _dot_sel(s_ref[s, p], coll_ref[...], 3)


def _rwkv(rw, shift_prev, wkv_prev, lp, consts, *, chunk, nb, cps):
    bsz, t, sw = rw.shape
    w = lp["rw_w0"].shape[-1]
    n_pairs = w // PAIR
    lora_in = lp["wlora_pad"].shape[0]
    tt = cps * chunk
    assert t % tt == 0 and bsz % nb == 0 and sw == 3 * w + lora_in + lp["rw_g_lora"].shape[0]
    vec = lambda a: a.reshape(1, -1)
    full = lambda shape: pl.BlockSpec(shape, lambda b, i: (0,) * len(shape))
    tri, ones_blk, dup, coll = consts
    state_spec = pl.BlockSpec((nb, n_pairs, PAIR, RW_HEAD), lambda b, i: (b, 0, 0, 0))
    in_specs = [
        pl.BlockSpec((nb, tt, sw), lambda b, i: (b, i, 0)),
        pl.BlockSpec((nb, 1, sw), lambda b, i: (b, 0, 0)),
        state_spec,
        full((1, sw)), full((1, w)), full((lora_in, w)), full((1, w)), full((lora_in, w)),
        full(lp["rw_g_lora"].shape), full((1, w)), full((1, w)), full((1, w)), full((1, w)), full((1, w)),
        full(tri.shape), full(ones_blk.shape), full(dup.shape), full(coll.shape),
    ]
    o, wkv_new = pl.pallas_call(
        functools.partial(_rwkv_kernel, lora_in=lora_in, chunk=chunk),
        grid=(bsz // nb, t // tt),
        in_specs=in_specs,
        out_specs=[pl.BlockSpec((nb, tt, w), lambda b, i: (b, i, 0)), state_spec],
        out_shape=[
            jax.ShapeDtypeStruct((bsz, t, w), F32),
            jax.ShapeDtypeStruct((bsz, n_pairs, PAIR, RW_HEAD), F32),
        ],
        scratch_shapes=[
            pltpu.VMEM((nb, n_pairs, PAIR, PAIR), F32),
            pltpu.VMEM((nb, 1, sw), F32),
            pltpu.VMEM((nb * tt, w), F32),
        ],
        compiler_params=_params("parallel", "arbitrary"),
        name="rwkv",
    )(rw, shift_prev.reshape(bsz, 1, sw), wkv_prev.reshape(bsz, n_pairs, PAIR, RW_HEAD),
      vec(lp["rw_mu"]), vec(lp["rw_w0"]), lp["wlora_pad"], vec(lp["rw_a0"]), lp["alora_pad"],
      lp["rw_g_lora"], vec(lp["rw_k_k"]), vec(lp["rw_k_a"]), vec(lp["rw_r_k"]),
      vec(lp["rw_lnx_w"]), vec(lp["rw_lnx_b"]), tri, ones_blk, dup, coll)
    return o, wkv_new.reshape(bsz, w // RW_HEAD, RW_HEAD, RW_HEAD)


def _rwkv_consts(chunk, nb, cps, w):
    idx = jnp.arange(nb * cps * chunk)
    tri = ((idx[:, None] >= idx[None, :]) & (idx[:, None] // chunk == idx[None, :] // chunk)).astype(BF16)
    hid = jnp.arange(w) // RW_HEAD
    ones_blk = (hid[:, None] == hid[None, :]).astype(BF16)
    dup = (jnp.arange(RW_HEAD)[:, None] == (jnp.arange(PAIR) % RW_HEAD)[None, :]).astype(BF16)
    return tri, ones_blk, dup, dup.T


def _lambda(lamv_ref, lam_init):
    lamv = lamv_ref[...]
    l1 = jnp.sum(lamv[0:1] * lamv[1:2], axis=1, keepdims=True)
    l2 = jnp.sum(lamv[2:3] * lamv[3:4], axis=1, keepdims=True)
    return jnp.exp(l1) - jnp.exp(l2) + lam_init


def _flash_kernel(qt_ref, kt_ref, lamv_ref, g_ref, q_ref, k_ref, v_ref, o_ref, m_ref, a_ref, *, lam_init):
    step = pl.program_id(2)
    qi = qt_ref[step]
    ki = kt_ref[step]
    tq = q_ref.shape[0]
    tk = k_ref.shape[0]
    hb = q_ref.shape[1] // LANES
    n = 2 * hb

    @pl.when(ki == 0)
    def _():
        m_ref[...] = jnp.full(m_ref.shape, -jnp.inf, F32)
        a_ref[...] = jnp.zeros(a_ref.shape, F32)

    def update(masked):
        lane = lax.broadcasted_iota(jnp.int32, (1, LANES), 1)
        qs, ks, vs = [], [], []
        for h in range(hb):
            cs = slice(h * LANES, (h + 1) * LANES)
            q = q_ref[:, cs]
            k = k_ref[:, cs].astype(BF16)
            v = v_ref[:, cs].astype(BF16)
            v_ext = jnp.concatenate([v, jnp.ones_like(v)], axis=1)
            qs += [jnp.where(lane < DA_SUB, q, 0.0).astype(BF16), jnp.where(lane >= DA_SUB, q, 0.0).astype(BF16)]
            ks += [k, k]
            vs += [v_ext, v_ext]
        s = [lax.dot_general(qs[j], ks[j], NT_DIMS, preferred_element_type=F32) for j in range(n)]
        if masked:
            causal = (lax.broadcasted_iota(jnp.int32, (tq, tk), 1) <= lax.broadcasted_iota(jnp.int32, (tq, tk), 0))
            s = [jnp.where(causal, t, -jnp.inf) for t in s]
        m_prev = [m_ref[j] for j in range(n)]
        m_new = [jnp.maximum(m_prev[j], jnp.max(s[j], axis=1, keepdims=True)) for j in range(n)]
        alpha = [jnp.exp(m_prev[j] - m_new[j]) for j in range(n)]
        p = [jnp.exp(s[j] - jnp.tile(m_new[j], (1, tk // LANES))).astype(BF16) for j in range(n)]
        pv = [jnp.dot(p[j], vs[j], preferred_element_type=F32) for j in range(n)]
        for j in range(n):
            a_ref[j] = jnp.concatenate([alpha[j], alpha[j]], axis=1) * a_ref[j] + pv[j]
            m_ref[j] = m_new[j]

    @pl.when(ki < qi)
    def _():
        update(False)

    @pl.when(ki == qi)
    def _():
        update(True)
        lam = _lambda(lamv_ref, lam_init)
        for h in range(hb):
            a1 = a_ref[2 * h]
            a2 = a_ref[2 * h + 1]
            o = a1[:, :LANES] / a1[:, LANES:] - lam * (a2[:, :LANES] / a2[:, LANES:])
            o_ref[:, h * LANES:(h + 1) * LANES] = _rms(o, SUBLN_EPS) * g_ref[...] * (1.0 - lam_init)


def _flash_attention(q, k, v, lamv, subln, *, bsz, t, lam_init, tile, hb):
    n, dw = v.shape
    heads = dw // LANES
    assert t % tile == 0 and n == bsz * t and heads % hb == 0
    nt = t // tile
    groups = heads // hb
    pairs = [(qi, ki) for qi in range(nt) for ki in range(qi + 1)]
    q_tab = jnp.array([p[0] for p in pairs], jnp.int32)
    k_tab = jnp.array([p[1] for p in pairs], jnp.int32)
    width = hb * LANES
    grid_spec = pltpu.PrefetchScalarGridSpec(
        num_scalar_prefetch=2,
        grid=(bsz, groups, len(pairs)),
        in_specs=[
            pl.BlockSpec(lamv.shape, lambda b, h, s, qt, kt: (0, 0)),
            pl.BlockSpec((1, LANES), lambda b, h, s, qt, kt: (0, 0)),
            pl.BlockSpec((tile, width), lambda b, h, s, qt, kt: (b * nt + qt[s], h)),
            pl.BlockSpec((tile, width), lambda b, h, s, qt, kt: (b * nt + kt[s], h)),
            pl.BlockSpec((tile, width), lambda b, h, s, qt, kt: (b * nt + kt[s], h)),
        ],
        out_specs=pl.BlockSpec((tile, width), lambda b, h, s, qt, kt: (b * nt + qt[s], h)),
        scratch_shapes=[pltpu.VMEM((2 * hb, tile, LANES), F32), pltpu.VMEM((2 * hb, tile, 2 * LANES), F32)],
    )
    return pl.pallas_call(
        functools.partial(_flash_kernel, lam_init=lam_init),
        grid_spec=grid_spec,
        out_shape=jax.ShapeDtypeStruct((n, dw), F32),
        compiler_params=_params("parallel", "parallel", "arbitrary"),
        name="flash_diff_attention",
    )(q_tab, k_tab, lamv, subln.reshape(1, LANES), q, k, v)


def _paged_kernel(pt_ref, lamv_ref, g_ref, wq_ref, kn_ref, vn_ref, *rest, n_pages, lam_init):
    kt_refs = rest[:n_pages]
    v_refs = rest[n_pages:2 * n_pages]
    o_ref = rest[2 * n_pages]
    tq = kn_ref.shape[0]
    heads = o_ref.shape[1] // LANES
    page = kt_refs[0].shape[1]
    rows = 2 * tq
    wq = wq_ref[...].astype(BF16)
    s_past = [jnp.dot(wq, kt[...].astype(BF16), preferred_element_type=F32) for kt in kt_refs]
    s_new = lax.dot_general(wq, kn_ref[...].astype(BF16), NT_DIMS, preferred_element_type=F32)
    qry_t = lax.broadcasted_iota(jnp.int32, s_new.shape, 0) % tq
    key_t = lax.broadcasted_iota(jnp.int32, s_new.shape, 1)
    s_new = jnp.where(key_t <= qry_t, s_new, -jnp.inf)
    m_past = s_past[0]
    for s in s_past[1:]:
        m_past = jnp.maximum(m_past, s)
    m = jnp.maximum(jnp.max(m_past, axis=1, keepdims=True), jnp.max(s_new, axis=1, keepdims=True))
    p_past = [jnp.exp(s - m) for s in s_past]
    p_new = jnp.exp(s_new - m)
    p_sum = p_past[0]
    for p in p_past[1:]:
        p_sum = p_sum + p
    den = jnp.sum(p_sum, axis=1, keepdims=True) + jnp.sum(p_new, axis=1, keepdims=True)
    p_past = [p.astype(BF16) for p in p_past]
    p_new = p_new.astype(BF16)
    vn = vn_ref[...].astype(BF16)
    lam = _lambda(lamv_ref, lam_init)
    for h in range(heads):
        rs = slice(h * rows, (h + 1) * rows)
        cs = slice(h * LANES, (h + 1) * LANES)
        acc = jnp.dot(p_new[rs], vn[:, cs], preferred_element_type=F32)
        for p, vr in zip(p_past, v_refs):
            v_h = vr[pl.ds(h, page, stride=heads), :]
            acc = acc + jnp.dot(p[rs], v_h.astype(BF16), preferred_element_type=F32)
        acc = acc / den[rs]
        o = acc[:tq] - lam * acc[tq:]
        o_ref[:, cs] = _rms(o, SUBLN_EPS) * g_ref[...] * (1.0 - lam_init)


def _paged_attention(wq, k_new, v_new, cache_kt, cache_v, page_table, lamv, subln, *, layer, lam_init):
    bsz, tq, dw = k_new.shape
    n_pages = page_table.shape[1]

    def page_spec(shape, j):
        return pl.BlockSpec((None, None) + shape, lambda b, pt: (layer, pt[b * n_pages + j]) + (0,) * len(shape))

    grid_spec = pltpu.PrefetchScalarGridSpec(
        num_scalar_prefetch=1,
        grid=(bsz,),
        in_specs=[
            pl.BlockSpec(lamv.shape, lambda b, pt: (0, 0)),
            pl.BlockSpec((1, LANES), lambda b, pt: (0, 0)),
            pl.BlockSpec((None,) + wq.shape[1:], lambda b, pt: (b, 0, 0)),
            pl.BlockSpec((None, tq, dw), lambda b, pt: (b, 0, 0)),
            pl.BlockSpec((None, tq, dw), lambda b, pt: (b, 0, 0)),
        ] + [page_spec(cache_kt.shape[2:], j) for j in range(n_pages)]
          + [page_spec(cache_v.shape[2:], j) for j in range(n_pages)],
        out_specs=pl.BlockSpec((None, tq, dw), lambda b, pt: (b, 0, 0)),
    )
    return pl.pallas_call(
        functools.partial(_paged_kernel, n_pages=n_pages, lam_init=lam_init),
        grid_spec=grid_spec,
        out_shape=jax.ShapeDtypeStruct((bsz, tq, dw), F32),
        compiler_params=_params("parallel"),
        name="paged_diff_attention",
    )(page_table.reshape(-1), lamv, subln.reshape(1, LANES), wq, k_new, v_new,
      *([cache_kt] * n_pages), *([cache_v] * n_pages))


def _merge_kernel(oa_ref, ob_ref, ga_ref, gb_ref, x_ref, wpa_ref, wpb_ref, wo_ref, g_ref, o_ref):
    m = ga_ref[...] * _dot(oa_ref[...], wpa_ref[...]) + gb_ref[...] * _dot(ob_ref[...], wpb_ref[...])
    a = _dot(m, wo_ref[...])
    o_ref[...] = x_ref[...] + _rms(a, NORM_EPS) * g_ref[...]


def _merge(o_a, o_b, gates, x, lp, *, tm):
    n, d = x.shape
    wd = o_a.shape[1]
    row = lambda width, j=0: pl.BlockSpec((tm, width), lambda i: (i, j))
    full = lambda shape: pl.BlockSpec(shape, lambda i: (0, 0))
    return pl.pallas_call(
        _merge_kernel,
        grid=(n // tm,),
        in_specs=[row(wd), row(wd), row(d, 0), row(d, 1), row(d),
                  full((wd, d)), full((wd, d)), full((d, d)), full((1, d))],
        out_specs=row(d),
        out_shape=jax.ShapeDtypeStruct((n, d), F32),
        compiler_params=_params("parallel"),
        name="merge",
    )(o_a, o_b, gates, gates, x, lp["w_pa"], lp["w_pb"], lp["w_o"], lp["norm_mix_post"].reshape(1, d))


def _ffn_kernel(*refs, seq_len, short, n_blocks):
    if short:
        x_ref, g1_ref, wu_ref, wz_ref, cw_ref, cb_ref, wd_ref, g2_ref, pad_ref, o_ref, u_ref, h_ref, acc_ref = refs
    else:
        (x_ref, g1_ref, wu_ref, wz_ref, cw_ref, cb_ref, wd_ref, g2_ref, cprev_ref,
         o_ref, last_ref, h_ref, acc_ref, carry_ref) = refs
    i = pl.program_id(0)
    c = pl.program_id(1)
    tm = x_ref.shape[0]

    @pl.when(c == 0)
    def _():
        h_ref[...] = (_rms(x_ref[...], NORM_EPS) * g1_ref[...]).astype(BF16)
        acc_ref[...] = jnp.zeros(acc_ref.shape, F32)

    if not short:
        @pl.when(i % (seq_len // tm) == 0)
        def _():
            carry_ref[c, pl.ds(SUBLANES - 2, 2), :] = cprev_ref[...]

    rb = tm // n_blocks
    cw = cw_ref[...]
    us, zs = [None] * n_blocks, [None] * n_blocks

    def up(r):
        h = h_ref[r * rb:(r + 1) * rb, :]
        us[r] = jnp.dot(h, wu_ref[...], preferred_element_type=F32)
        zs[r] = jnp.dot(h, wz_ref[...], preferred_element_type=F32)

    def gate(r):
        u, z = us[r], zs[r]
        rows = slice(r * rb, (r + 1) * rb)
        row = lax.broadcasted_iota(jnp.int32, (rb, 1), 0)
        if short:
            pad = pad_ref[rows, :]
            t = row % seq_len
            s1 = jnp.where(t == 0, pltpu.roll(pad, rb - 1, 0), pltpu.roll(u, 1, 0))
            s2 = jnp.where(t < 2, pad, pltpu.roll(u, 2, 0))
            u_ref[rows, :] = u
        else:
            before = carry_ref[c, pl.ds(SUBLANES - 2, 2), :] if r == 0 else us[r - 1][rb - 2:rb]
            p2 = before[0:1]
            p1 = before[1:2]
            s1 = jnp.where(row == 0, p1, pltpu.roll(u, 1, 0))
            s2 = jnp.where(row == 0, p2, jnp.where(row == 1, p1, pltpu.roll(u, 2, 0)))
        uc = cb_ref[...] + s2 * cw[0:1] + s1 * cw[1:2] + u * cw[2:3]
        act = 0.5 * uc * (1.0 + lax.erf(uc * (2.0 ** -0.5))) * z
        acc_ref[rows, :] += jnp.dot(act.astype(BF16), wd_ref[...], preferred_element_type=F32)

    up(0)
    for r in range(n_blocks):
        if r + 1 < n_blocks:
            up(r + 1)
        gate(r)
    if not short:
        carry_ref[c] = us[-1][rb - SUBLANES:rb]
        last_ref[...] = us[-1][rb - 2:rb]

    @pl.when(c == pl.num_programs(1) - 1)
    def _():
        o_ref[...] = x_ref[...] + _rms(acc_ref[...], NORM_EPS) * g2_ref[...]


def _ffn(x, conv_prev, lp, *, bsz, t, tm, tf):
    n, d = x.shape
    dff = lp["w_down"].shape[0]
    kw = lp["conv_w"].shape[0]
    assert kw == 3 and dff % tf == 0 and n % tm == 0 and t >= 2
    nc = dff // tf
    short = t < tm
    unit = t if short else SUBLANES
    n_blocks = FFN_ROW_BLOCKS if tm % (FFN_ROW_BLOCKS * unit) == 0 else 1
    in_specs = [
        pl.BlockSpec((tm, d), lambda i, c: (i, 0)),
        pl.BlockSpec((1, d), lambda i, c: (0, 0)),
        pl.BlockSpec((d, tf), lambda i, c: (0, c)),
        pl.BlockSpec((d, tf), lambda i, c: (0, nc + c)),
        pl.BlockSpec((kw, tf), lambda i, c: (0, c)),
        pl.BlockSpec((1, tf), lambda i, c: (0, c)),
        pl.BlockSpec((tf, d), lambda i, c: (c, 0)),
        pl.BlockSpec((1, d), lambda i, c: (0, 0)),
    ]
    args = [x, lp["norm_ffn_pre"].reshape(1, d), lp["w_up"], lp["w_up"], lp["conv_w"],
            lp["conv_b"].reshape(1, dff), lp["w_down"], lp["norm_ffn_post"].reshape(1, d)]
    x_spec = pl.BlockSpec((tm, d), lambda i, c: (i, 0))
    scratch = [pltpu.VMEM((tm, d), BF16), pltpu.VMEM((tm, d), F32)]
    if short:
        assert tm % t == 0
        pad = jnp.concatenate([conv_prev, jnp.zeros((bsz, t - 2, dff), F32)], axis=1).reshape(n, dff)
        in_specs.append(pl.BlockSpec((tm, tf), lambda i, c: (i, c)))
        args.append(pad)
        out_specs = [x_spec, pl.BlockSpec((tm, tf), lambda i, c: (i, c))]
        out_shape = [jax.ShapeDtypeStruct((n, d), F32), jax.ShapeDtypeStruct((n, dff), F32)]
    else:
        assert t % tm == 0
        tiles = t // tm
        in_specs.append(pl.BlockSpec((None, 2, tf), lambda i, c: (i // tiles, 0, c)))
        args.append(conv_prev)
        out_specs = [x_spec, pl.BlockSpec((None, 2, tf), lambda i, c: (i, 0, c))]
        out_shape = [jax.ShapeDtypeStruct((n, d), F32), jax.ShapeDtypeStruct((n // tm, 2, dff), F32)]
        scratch.append(pltpu.VMEM((nc, SUBLANES, tf), F32))
    x_new, aux = pl.pallas_call(
        functools.partial(_ffn_kernel, seq_len=t, short=short, n_blocks=n_blocks),
        grid=(n // tm, nc),
        in_specs=in_specs,
        out_specs=out_specs,
        out_shape=out_shape,
        scratch_shapes=scratch,
        compiler_params=_params("arbitrary", "arbitrary"),
        name="ffn",
    )(*args)
    if short:
        conv_new = aux.reshape(bsz, t, dff)[:, t - 2:]
    else:
        conv_new = aux.reshape(bsz, t // tm, 2, dff)[:, -1]
    return x_new, conv_new


def _rope_tables(pos, rows, width):
    inv = jnp.exp(-math.log(ROPE_THETA) * jnp.arange(HALF_ROT, dtype=F32) * (2.0 / ROT_DIM))
    ang = pos.astype(F32)[:, None] * inv[None, :]
    cos, sin = jnp.cos(ang), jnp.sin(ang)
    t = pos.shape[0]
    zeros = jnp.zeros((t, HALF_ROT), F32)
    rest = DA_SUB - ROT_DIM
    cos_t = jnp.concatenate([cos, cos, jnp.ones((t, rest), F32)], axis=1)
    sup_t = jnp.concatenate([-sin, zeros, jnp.zeros((t, rest), F32)], axis=1)
    sdn_t = jnp.concatenate([zeros, sin, jnp.zeros((t, rest), F32)], axis=1)
    return tuple(jnp.tile(tab, (rows // t, width // DA_SUB)) for tab in (cos_t, sup_t, sdn_t))


def _tiles(n, t):
    tm = next(c for c in (1024, 512, 256, 128) if n % c == 0 and (t % c == 0 or c % t == 0))
    return tm, min(tm, 512), 256


def _rwkv_blocking(bsz, t):
    chunk = min(t, 64)
    nb = 2 if chunk == 64 else 8
    cps = next(n for n in (4, 2, 1) if t % (n * chunk) == 0)
    return chunk, math.gcd(nb, bsz), cps


def _group_layer(x, bsz, t, lp, state, rope, consts, attention):
    shift_prev, wkv_prev, conv_prev = state
    tm, tm_merge, tf = _tiles(x.shape[0], t)
    chunk, nb, cps = _rwkv_blocking(bsz, t)
    g = lp["norm_mix_pre"]
    sw = lp["rw_mu"].shape[0]
    rw, gates = _rw_gates(x, g, lp["w_rwg"], sw, tm=tm_merge)
    q, k, v = _qkv(x, g, lp["w_qkv"], rope, tm=tm)
    o_a, wkv_new = _rwkv(rw.reshape(bsz, t, sw), shift_prev, wkv_prev, lp, consts, chunk=chunk, nb=nb, cps=cps)
    dw = v.shape[1]
    o_b = attention(q, k, v)
    x = _merge(o_a.reshape(bsz * t, -1), o_b, gates, x, lp, tm=tm_merge)
    x, conv_new = _ffn(x, conv_prev, lp, bsz=bsz, t=t, tm=tm, tf=tf)
    k_new = k.reshape(bsz, t, dw // (2 * DA_SUB), 2, DA_SUB)
    v_new = v.reshape(bsz, t, dw // (2 * DA_SUB), 2 * DA_SUB)
    shift_new = rw.reshape(bsz, t, sw)[:, t - 1]
    return x, (k_new, v_new, wkv_new, shift_new, conv_new)


def kernel(x_prompt, x_sample, cache_k, cache_v, state_wkv, state_shift, state_conv, page_table, norm_mix_pre, norm_mix_post, norm_ffn_pre, norm_ffn_post, w_in, rw_mu, rw_w0, rw_w_lora, rw_a0, rw_a_lora, rw_g_lora, rw_k_k, rw_k_a, rw_r_k, rw_lnx_w, rw_lnx_b, da_lam_q1, da_lam_k1, da_lam_q2, da_lam_k2, da_subln, w_pa, w_pb, w_o, w_up, conv_w, conv_b, w_down):
    bp, tp, d = x_prompt.shape
    bs, ts, _ = x_sample.shape
    depth = w_in.shape[0]
    rww = rw_w0.shape[1]
    sw = rw_mu.shape[1]
    wl, al = rw_w_lora.shape[1], rw_a_lora.shape[1]
    dw = w_pb.shape[1]
    dff = w_down.shape[1]
    n_pages, page = page_table.shape[1], cache_k.shape[2]
    past = n_pages * page
    assert wl + al == LANES and sw == 3 * rww + wl + al + rw_g_lora.shape[1] and rww % PAIR == 0
    assert dw % LANES == 0 and w_in.shape[2] == sw + 3 * dw + 2 * d

    rope_p = _rope_tables(jnp.arange(tp, dtype=jnp.int32), max(tp, _tiles(bp * tp, tp)[0]), dw)
    rope_s = _rope_tables(past + jnp.arange(ts, dtype=jnp.int32), max(ts, _tiles(bs * ts, ts)[0]), dw)
    consts_p = _rwkv_consts(*_rwkv_blocking(bp, tp), rww)
    consts_s = _rwkv_consts(*_rwkv_blocking(bs, ts), rww)
    cache_kt = jnp.transpose(cache_k, (0, 1, 3, 4, 5, 2)).reshape(depth, cache_k.shape[1], dw, page)
    cache_vr = cache_v.reshape(depth, cache_v.shape[1], page * (dw // LANES), LANES)
    sub_sel = (jnp.arange(dw // DA_SUB)[:, None] == jnp.arange(dw)[None, :] // DA_SUB).astype(F32)

    xp = x_prompt.reshape(bp * tp, d)
    xs = x_sample.reshape(bs * ts, d)
    state_p = (jnp.zeros((bp, sw), F32), jnp.zeros((bp, rww // RW_HEAD, RW_HEAD, RW_HEAD), F32),
               jnp.zeros((bp, 2, dff), F32))
    outs_p, outs_s = [], []
    for l in range(depth):
        wi = w_in[l].astype(BF16)
        lp = {
            "norm_mix_pre": norm_mix_pre[l], "norm_mix_post": norm_mix_post[l],
            "norm_ffn_pre": norm_ffn_pre[l], "norm_ffn_post": norm_ffn_post[l],
            "w_rwg": jnp.concatenate([wi[:, :sw], wi[:, sw + 3 * dw:]], axis=1), "w_qkv": wi[:, sw:sw + 3 * dw],
            "rw_mu": rw_mu[l], "rw_w0": rw_w0[l], "rw_a0": rw_a0[l],
            "wlora_pad": jnp.concatenate([rw_w_lora[l], jnp.zeros((al, rww), F32)], axis=0).astype(BF16),
            "alora_pad": jnp.concatenate([jnp.zeros((wl, rww), F32), rw_a_lora[l]], axis=0).astype(BF16),
            "rw_g_lora": rw_g_lora[l].astype(BF16),
            "rw_k_k": rw_k_k[l], "rw_k_a": rw_k_a[l], "rw_r_k": rw_r_k[l].reshape(-1),
            "rw_lnx_w": rw_lnx_w[l], "rw_lnx_b": rw_lnx_b[l],
            "w_pa": w_pa[l].astype(BF16), "w_pb": w_pb[l].astype(BF16), "w_o": w_o[l].astype(BF16),
            "w_up": w_up[l].astype(BF16), "conv_w": conv_w[l], "conv_b": conv_b[l],
            "w_down": w_down[l].astype(BF16),
        }
        lam_init = 0.8 - 0.6 * math.exp(-0.3 * l)
        lamv = jnp.stack([da_lam_q1[l], da_lam_k1[l], da_lam_q2[l], da_lam_k2[l]])
        subln = da_subln[l]

        def attn_prompt(q, k, v):
            return _flash_attention(q, k, v, lamv, subln, bsz=bp, t=tp, lam_init=lam_init, tile=min(tp, 512), hb=4)

        def attn_sample(q, k, v):
            wq = (q.reshape(bs, 1, ts, dw) * sub_sel[None, :, None, :]).reshape(bs, -1, dw)
            o = _paged_attention(wq, k.reshape(bs, ts, dw), v.reshape(bs, ts, dw), cache_kt, cache_vr,
                                 page_table, lamv, subln, layer=l, lam_init=lam_init)
            return o.reshape(bs * ts, dw)

        xp, out_p = _group_layer(xp, bp, tp, lp, state_p, rope_p, consts_p, attn_prompt)
        xs, out_s = _group_layer(xs, bs, ts, lp, (state_shift[l], state_wkv[l], state_conv[l]), rope_s, consts_s,
                                 attn_sample)
        outs_p.append(out_p)
        outs_s.append(out_s)

    stack = lambda outs, j: jnp.stack([o[j] for o in outs])
    return ((xp.reshape(bp, tp, d), xs.reshape(bs, ts, d))
            + tuple(stack(outs_p, j) for j in range(5)) + tuple(stack(outs_s, j) for j in range(5)))
```

```python
import functools
import math

import jax
import jax.numpy as jnp
from jax import lax
from jax.experimental import pallas as pl
from jax.experimental.pallas import tpu as pltpu

F32 = jnp.float32
BF16 = jnp.bfloat16

RW_HEAD = 64
DA_SUB = 64
ROT_DIM = 16
ROPE_THETA = 500000.0
NORM_EPS = 1e-6
LNX_EPS = 64e-5
SUBLN_EPS = 1e-5
KK_EPS = 1e-12

LANES = 128
SUBLANES = 8
VMEM_LIMIT_BYTES = 48 * 1024 * 1024

FFN_ROW_BLOCKS = 2
PAIR = 2 * RW_HEAD
HALF_ROT = ROT_DIM // 2
TN_DIMS = (((0,), (0,)), ((), ()))
NT_DIMS = (((1,), (1,)), ((), ()))


def _params(*semantics):
    return pltpu.CompilerParams(dimension_semantics=semantics, vmem_limit_bytes=VMEM_LIMIT_BYTES)


def _sigmoid(x):
    return 1.0 / (1.0 + jnp.exp(-x))


def _rms(x, eps):
    return x * lax.rsqrt(jnp.mean(x * x, axis=-1, keepdims=True) + eps)


def _dot(a, b):
    return jnp.dot(a.astype(BF16), b.astype(BF16), preferred_element_type=F32)


def _dot_nt(a, b):
    return lax.dot_general(a.astype(BF16), b.astype(BF16), NT_DIMS, preferred_element_type=F32)


def _dot_tn(a, b):
    return lax.dot_general(a.astype(BF16), b.astype(BF16), TN_DIMS, preferred_element_type=F32)


def _split(x, parts):
    pieces = []
    rest = x
    for _ in range(parts):
        piece = rest.astype(BF16)
        pieces.append(piece)
        rest = rest - piece.astype(F32)
    return pieces


def _sel_dot(sel, x, parts):
    out = None
    for piece in _split(x, parts):
        term = jnp.dot(sel, piece, preferred_element_type=F32)
        out = term if out is None else out + term
    return out


def _dot_sel(x, sel, parts):
    out = None
    for piece in _split(x, parts):
        term = jnp.dot(piece, sel, preferred_element_type=F32)
        out = term if out is None else out + term
    return out


def _rw_gates_kernel(x_ref, g_ref, w_ref, rw_ref, gates_ref, *, tn):
    h = (_rms(x_ref[...], NORM_EPS) * g_ref[...]).astype(BF16)
    sw = rw_ref.shape[1]
    for lo in range(0, sw, tn):
        rw_ref[:, lo:lo + tn] = jnp.dot(h, w_ref[:, lo:lo + tn], preferred_element_type=F32)
    for lo in range(0, gates_ref.shape[1], tn):
        gates_ref[:, lo:lo + tn] = _sigmoid(jnp.dot(h, w_ref[:, sw + lo:sw + lo + tn], preferred_element_type=F32))


def _rw_gates(x, g, w, sw, *, tm):
    n, d = x.shape
    gw = w.shape[1] - sw
    tn = math.gcd(sw, gw)
    assert n % tm == 0 and tn % LANES == 0
    return pl.pallas_call(
        functools.partial(_rw_gates_kernel, tn=tn),
        grid=(n // tm,),
        in_specs=[
            pl.BlockSpec((tm, d), lambda i: (i, 0)),
            pl.BlockSpec((1, d), lambda i: (0, 0)),
            pl.BlockSpec((d, sw + gw), lambda i: (0, 0)),
        ],
        out_specs=[pl.BlockSpec((tm, sw), lambda i: (i, 0)), pl.BlockSpec((tm, gw), lambda i: (i, 0))],
        out_shape=[jax.ShapeDtypeStruct((n, sw), F32), jax.ShapeDtypeStruct((n, gw), F32)],
        compiler_params=_params("parallel"),
        name="rw_gates",
    )(x, g.reshape(1, d), w)


def _qkv_kernel(x_ref, g_ref, w_ref, cos_ref, sup_ref, sdn_ref, q_ref, k_ref, v_ref):
    dw = q_ref.shape[1]
    h = (_rms(x_ref[...], NORM_EPS) * g_ref[...]).astype(BF16)

    def rope(t):
        return (t * cos_ref[...] + pltpu.roll(t, HALF_ROT, 1) * sdn_ref[...]
                + pltpu.roll(t, dw - HALF_ROT, 1) * sup_ref[...])

    q = jnp.dot(h, w_ref[:, 0:dw], preferred_element_type=F32)
    k = jnp.dot(h, w_ref[:, dw:2 * dw], preferred_element_type=F32)
    v_ref[...] = jnp.dot(h, w_ref[:, 2 * dw:3 * dw], preferred_element_type=F32)
    q_ref[...] = rope(q) * (DA_SUB ** -0.5)
    k_ref[...] = rope(k)


def _qkv(x, g, w, rope, *, tm):
    n, d = x.shape
    dw = w.shape[1] // 3
    tab_blocks = rope[0].shape[0] // tm
    assert n % tm == 0 and rope[0].shape == (tab_blocks * tm, dw)
    row = pl.BlockSpec((tm, dw), lambda i: (i, 0))
    return pl.pallas_call(
        _qkv_kernel,
        grid=(n // tm,),
        in_specs=[
            pl.BlockSpec((tm, d), lambda i: (i, 0)),
            pl.BlockSpec((1, d), lambda i: (0, 0)),
            pl.BlockSpec((d, 3 * dw), lambda i: (0, 0)),
        ] + [pl.BlockSpec((tm, dw), lambda i: (i % tab_blocks, 0))] * 3,
        out_specs=[row, row, row],
        out_shape=[jax.ShapeDtypeStruct((n, dw), F32)] * 3,
        compiler_params=_params("parallel"),
        name="qkv_rope",
    )(x, g.reshape(1, d), w, *rope)


def _rwkv_kernel(rw_ref, shift_ref, wkv_ref, mu_ref, w0_ref, wlora_ref, a0_ref, alora_ref, glora_ref,
                 kk_ref, ka_ref, rk_ref, lnw_ref, lnb_ref, tri_ref, ones_ref, dup_ref, coll_ref,
                 o_ref, wkv_out_ref, s_ref, prev_ref, y_ref, *, lora_in, chunk):
    i = pl.program_id(1)
    nb, span, sw = rw_ref.shape
    c = chunk
    cps = span // c
    w = w0_ref.shape[1]
    n_pairs = w // PAIR
    c2 = 2 * c
    chains = [(s, p) for s in range(nb * cps) for p in range(n_pairs)]
    n = len(chains)

    @pl.when(i == 0)
    def _():
        prev_ref[...] = shift_ref[...]
        blk = (lax.broadcasted_iota(jnp.int32, (PAIR, PAIR), 0) // RW_HEAD
               == lax.broadcasted_iota(jnp.int32, (PAIR, PAIR), 1) // RW_HEAD)
        for s in range(nb):
            for p in range(n_pairs):
                s_ref[s, p] = jnp.where(blk, _dot_sel(wkv_ref[s, p], dup_ref[...], 3), 0.0)

    rw = rw_ref[...].reshape(nb * span, sw)
    row = lax.broadcasted_iota(jnp.int32, (nb * span, 1), 0)
    prev = pltpu.roll(rw, 1, 0)
    for s in range(nb):
        prev = jnp.where(row == s * span, prev_ref[s], prev)
        prev_ref[s] = rw[(s + 1) * span - 1:(s + 1) * span, :]
    rs = rw + mu_ref[...] * (prev - rw)
    r = rs[:, 0:w]
    k = rs[:, w:2 * w]
    v = rs[:, 2 * w:3 * w]
    wa = rs[:, 3 * w:3 * w + lora_in]
    gl = rs[:, 3 * w + lora_in:]

    z = -(w0_ref[...] + _dot(jnp.tanh(wa), wlora_ref[...]))
    w_log = -(jnp.maximum(z, 0.0) + jnp.log1p(jnp.exp(-jnp.abs(z)))) - 0.5
    lw = -jnp.exp(w_log)
    a = _sigmoid(a0_ref[...] + _dot(wa, alora_ref[...]))
    g = _dot(_sigmoid(gl), glora_ref[...])

    def head_sum(t):
        return _dot_sel(t, ones_ref[...], 1)

    kk = k * kk_ref[...]
    kk = kk * lax.rsqrt(head_sum(kk * kk) + KK_EPS)
    km = k * (1.0 + (a - 1.0) * ka_ref[...])
    b = kk * a

    cum = _sel_dot(tri_ref[...], lw, 3)
    tots = [cum[(s + 1) * c - 1:(s + 1) * c, :] for s in range(nb * cps)]
    tot = jnp.concatenate([jnp.broadcast_to(t, (c, w)) for t in tots], axis=0)
    kk_t = kk * jnp.exp(cum - lw)
    r_t = r * jnp.exp(cum)
    inv = jnp.exp(-cum)
    k_h = km * inv
    b_h = b * inv
    tail = jnp.exp(tot - cum)
    k_p = km * tail
    b_p = b * tail
    gams = [jnp.exp(t) for t in tots]

    tt = lax.broadcasted_iota(jnp.int32, (c, c2), 0)
    ss = lax.broadcasted_iota(jnp.int32, (c, c2), 1) % c
    strict = ss < tt
    incl = ss <= tt
    eye = (ss == tt).astype(F32)
    keep_lane = (lax.broadcasted_iota(jnp.int32, (c2, PAIR), 0) // c
                 == lax.broadcasted_iota(jnp.int32, (c2, PAIR), 1) // RW_HEAD)
    keep_time = (lax.broadcasted_iota(jnp.int32, (c2, c2), 0) // c
                 == lax.broadcasted_iota(jnp.int32, (c2, c2), 1) // c)
    blk = (lax.broadcasted_iota(jnp.int32, (PAIR, PAIR), 0) // RW_HEAD
           == lax.broadcasted_iota(jnp.int32, (PAIR, PAIR), 1) // RW_HEAD)

    def stack(t, keep):
        return jnp.where(keep, jnp.concatenate([t, t], axis=0), 0.0).astype(BF16)

    def pieces(t):
        return [t[s * c:(s + 1) * c, p * PAIR:(p + 1) * PAIR] for s, p in chains]

    kk_n, r_n, v_n = pieces(kk_t), pieces(r_t), pieces(v)
    kp_n, bp_n = pieces(k_p), pieces(b_p)
    y_k = [stack(t, keep_lane) for t in pieces(k_h)]
    y_b = [stack(t, keep_lane) for t in pieces(b_h)]
    v_s = [stack(t, keep_lane) for t in v_n]
    lhs = [jnp.concatenate([kk_n[j], r_n[j]], axis=0) for j in range(n)]
    g_k = [_dot_nt(lhs[j], y_k[j]) for j in range(n)]
    g_b = [_dot_nt(lhs[j], y_b[j]) for j in range(n)]
    low = [jnp.where(strict, t[:c], 0.0) for t in g_b]
    t_inv = [eye - jnp.where((tt // 2) == (ss // 2), t, 0.0) for t in low]
    m = 2
    while m < c:
        sel = ((tt // (2 * m)) == (ss // (2 * m))) & ((tt // m) != (ss // m))
        half = [_dot(t_inv[j], stack(jnp.where(sel, low[j], 0.0), keep_time)) for j in range(n)]
        t_inv = [t_inv[j] - _dot(half[j], stack(t_inv[j], keep_time)) for j in range(n)]
        m *= 2
    w1 = [_dot(jnp.where(strict, g_k[j][:c], 0.0), v_s[j]) for j in range(n)]
    pq = [_dot(t_inv[j], jnp.concatenate([stack(kk_n[j], keep_lane), stack(w1[j], keep_lane)], axis=1))
          for j in range(n)]
    state = {(s, p): s_ref[s, p] for s in range(nb) for p in range(n_pairs)}
    for q in range(cps):
        now = [j for j, (s, p) in enumerate(chains) if s % cps == q]
        s0 = {j: state[(chains[j][0] // cps, chains[j][1])] for j in now}
        u = {j: _dot_nt(pq[j][:, :PAIR], s0[j]) + pq[j][:, PAIR:] for j in now}
        y = {j: _dot_nt(r_n[j], s0[j]) + _dot(jnp.where(incl, g_k[j][c:], 0.0), v_s[j])
             - _dot(jnp.where(incl, g_b[j][c:], 0.0), stack(u[j], keep_lane)) for j in now}
        for j in now:
            s, p = chains[j]
            y_ref[s * c:(s + 1) * c, p * PAIR:(p + 1) * PAIR] = y[j]
            vu = jnp.concatenate([v_n[j], u[j]], axis=0)
            kb = jnp.concatenate([kp_n[j], -bp_n[j]], axis=0)
            state[(s // cps, p)] = (s0[j] * gams[s][:, p * PAIR:(p + 1) * PAIR]
                                    + jnp.where(blk, _dot_tn(vu, kb), 0.0))
    for (s, p), value in state.items():
        s_ref[s, p] = value

    y = y_ref[...]
    inv_n = 1.0 / RW_HEAD
    d = y - head_sum(y) * inv_n
    yn = d * lax.rsqrt(head_sum(d * d) * inv_n + LNX_EPS) * lnw_ref[...] + lnb_ref[...]
    bonus = head_sum(r * km * rk_ref[...]) * v
    o_ref[...] = ((yn + bonus) * g).reshape(nb, span, w)

    @pl.when(i == pl.num_programs(1) - 1)
    def _():
        for s in range(nb):
            for p in range(n_pairs):
                wkv_out_ref[s, p] = _dot_sel(s_ref[s, p], coll_ref[...], 3)


def _rwkv(rw, shift_prev, wkv_prev, lp, consts, *, chunk, nb, cps):
    bsz, t, sw = rw.shape
    w = lp["rw_w0"].shape[-1]
    n_pairs = w // PAIR
    lora_in = lp["wlora_pad"].shape[0]
    tt = cps * chunk
    assert t % tt == 0 and bsz % nb == 0 and sw == 3 * w + lora_in + lp["rw_g_lora"].shape[0]
    vec = lambda a: a.reshape(1, -1)
    full = lambda shape: pl.BlockSpec(shape, lambda b, i: (0,) * len(shape))
    tri, ones_blk, dup, coll = consts
    state_spec = pl.BlockSpec((nb, n_pairs, PAIR, RW_HEAD), lambda b, i: (b, 0, 0, 0))
    in_specs = [
        pl.BlockSpec((nb, tt, sw), lambda b, i: (b, i, 0)),
        pl.BlockSpec((nb, 1, sw), lambda b, i: (b, 0, 0)),
        state_spec,
        full((1, sw)), full((1, w)), full((lora_in, w)), full((1, w)), full((lora_in, w)),
        full(lp["rw_g_lora"].shape), full((1, w)), full((1, w)), full((1, w)), full((1, w)), full((1, w)),
        full(tri.shape), full(ones_blk.shape), full(dup.shape), full(coll.shape),
    ]
    o, wkv_new = pl.pallas_call(
        functools.partial(_rwkv_kernel, lora_in=lora_in, chunk=chunk),
        grid=(bsz // nb, t // tt),
        in_specs=in_specs,
        out_specs=[pl.BlockSpec((nb, tt, w), lambda b, i: (b, i, 0)), state_spec],
        out_shape=[
            jax.ShapeDtypeStruct((bsz, t, w), F32),
            jax.ShapeDtypeStruct((bsz, n_pairs, PAIR, RW_HEAD), F32),
        ],
        scratch_shapes=[
            pltpu.VMEM((nb, n_pairs, PAIR, PAIR), F32),
            pltpu.VMEM((nb, 1, sw), F32),
            pltpu.VMEM((nb * tt, w), F32),
        ],
        compiler_params=_params("parallel", "arbitrary"),
        name="rwkv",
    )(rw, shift_prev.reshape(bsz, 1, sw), wkv_prev.reshape(bsz, n_pairs, PAIR, RW_HEAD),
      vec(lp["rw_mu"]), vec(lp["rw_w0"]), lp["wlora_pad"], vec(lp["rw_a0"]), lp["alora_pad"],
      lp["rw_g_lora"], vec(lp["rw_k_k"]), vec(lp["rw_k_a"]), vec(lp["rw_r_k"]),
      vec(lp["rw_lnx_w"]), vec(lp["rw_lnx_b"]), tri, ones_blk, dup, coll)
    return o, wkv_new.reshape(bsz, w // RW_HEAD, RW_HEAD, RW_HEAD)


def _rwkv_consts(chunk, nb, cps, w):
    idx = jnp.arange(nb * cps * chunk)
    tri = ((idx[:, None] >= idx[None, :]) & (idx[:, None] // chunk == idx[None, :] // chunk)).astype(BF16)
    hid = jnp.arange(w) // RW_HEAD
    ones_blk = (hid[:, None] == hid[None, :]).astype(BF16)
    dup = (jnp.arange(RW_HEAD)[:, None] == (jnp.arange(PAIR) % RW_HEAD)[None, :]).astype(BF16)
    return tri, ones_blk, dup, dup.T


def _lambda(lamv_ref, lam_init):
    lamv = lamv_ref[...]
    l1 = jnp.sum(lamv[0:1] * lamv[1:2], axis=1, keepdims=True)
    l2 = jnp.sum(lamv[2:3] * lamv[3:4], axis=1, keepdims=True)
    return jnp.exp(l1) - jnp.exp(l2) + lam_init


def _flash_kernel(qt_ref, kt_ref, lamv_ref, g_ref, q_ref, k_ref, v_ref, o_ref, m_ref, a_ref, *, lam_init):
    step = pl.program_id(2)
    qi = qt_ref[step]
    ki = kt_ref[step]
    tq = q_ref.shape[0]
    tk = k_ref.shape[0]
    hb = q_ref.shape[1] // LANES
    n = 2 * hb

    @pl.when(ki == 0)
    def _():
        m_ref[...] = jnp.full(m_ref.shape, -jnp.inf, F32)
        a_ref[...] = jnp.zeros(a_ref.shape, F32)

    def update(masked):
        lane = lax.broadcasted_iota(jnp.int32, (1, LANES), 1)
        qs, ks, vs = [], [], []
        for h in range(hb):
            cs = slice(h * LANES, (h + 1) * LANES)
            q = q_ref[:, cs]
            k = k_ref[:, cs].astype(BF16)
            v = v_ref[:, cs].astype(BF16)
            v_ext = jnp.concatenate([v, jnp.ones_like(v)], axis=1)
            qs += [jnp.where(lane < DA_SUB, q, 0.0).astype(BF16), jnp.where(lane >= DA_SUB, q, 0.0).astype(BF16)]
            ks += [k, k]
            vs += [v_ext, v_ext]
        s = [lax.dot_general(qs[j], ks[j], NT_DIMS, preferred_element_type=F32) for j in range(n)]
        if masked:
            causal = (lax.broadcasted_iota(jnp.int32, (tq, tk), 1) <= lax.broadcasted_iota(jnp.int32, (tq, tk), 0))
            s = [jnp.where(causal, t, -jnp.inf) for t in s]
        m_prev = [m_ref[j] for j in range(n)]
        m_new = [jnp.maximum(m_prev[j], jnp.max(s[j], axis=1, keepdims=True)) for j in range(n)]
        alpha = [jnp.exp(m_prev[j] - m_new[j]) for j in range(n)]
        p = [jnp.exp(s[j] - jnp.tile(m_new[j], (1, tk // LANES))).astype(BF16) for j in range(n)]
        pv = [jnp.dot(p[j], vs[j], preferred_element_type=F32) for j in range(n)]
        for j in range(n):
            a_ref[j] = jnp.concatenate([alpha[j], alpha[j]], axis=1) * a_ref[j] + pv[j]
            m_ref[j] = m_new[j]

    @pl.when(ki < qi)
    def _():
        update(False)

    @pl.when(ki == qi)
    def _():
        update(True)
        lam = _lambda(lamv_ref, lam_init)
        for h in range(hb):
            a1 = a_ref[2 * h]
            a2 = a_ref[2 * h + 1]
            o = a1[:, :LANES] / a1[:, LANES:] - lam * (a2[:, :LANES] / a2[:, LANES:])
            o_ref[:, h * LANES:(h + 1) * LANES] = _rms(o, SUBLN_EPS) * g_ref[...] * (1.0 - lam_init)


def _flash_attention(q, k, v, lamv, subln, *, bsz, t, lam_init, tile, hb):
    n, dw = v.shape
    heads = dw // LANES
    assert t % tile == 0 and n == bsz * t and heads % hb == 0
    nt = t // tile
    groups = heads // hb
    pairs = [(qi, ki) for qi in range(nt) for ki in range(qi + 1)]
    q_tab = jnp.array([p[0] for p in pairs], jnp.int32)
    k_tab = jnp.array([p[1] for p in pairs], jnp.int32)
    width = hb * LANES
    grid_spec = pltpu.PrefetchScalarGridSpec(
        num_scalar_prefetch=2,
        grid=(bsz, groups, len(pairs)),
        in_specs=[
            pl.BlockSpec(lamv.shape, lambda b, h, s, qt, kt: (0, 0)),
            pl.BlockSpec((1, LANES), lambda b, h, s, qt, kt: (0, 0)),
            pl.BlockSpec((tile, width), lambda b, h, s, qt, kt: (b * nt + qt[s], h)),
            pl.BlockSpec((tile, width), lambda b, h, s, qt, kt: (b * nt + kt[s], h)),
            pl.BlockSpec((tile, width), lambda b, h, s, qt, kt: (b * nt + kt[s], h)),
        ],
        out_specs=pl.BlockSpec((tile, width), lambda b, h, s, qt, kt: (b * nt + qt[s], h)),
        scratch_shapes=[pltpu.VMEM((2 * hb, tile, LANES), F32), pltpu.VMEM((2 * hb, tile, 2 * LANES), F32)],
    )
    return pl.pallas_call(
        functools.partial(_flash_kernel, lam_init=lam_init),
        grid_spec=grid_spec,
        out_shape=jax.ShapeDtypeStruct((n, dw), F32),
        compiler_params=_params("parallel", "parallel", "arbitrary"),
        name="flash_diff_attention",
    )(q_tab, k_tab, lamv, subln.reshape(1, LANES), q, k, v)


def _paged_kernel(pt_ref, lamv_ref, g_ref, wq_ref, kn_ref, vn_ref, *rest, n_pages, lam_init):
    kt_refs = rest[:n_pages]
    v_refs = rest[n_pages:2 * n_pages]
    o_ref = rest[2 * n_pages]
    tq = kn_ref.shape[0]
    heads = o_ref.shape[1] // LANES
    page = kt_refs[0].shape[1]
    rows = 2 * tq
    wq = wq_ref[...].astype(BF16)
    s_past = [jnp.dot(wq, kt[...].astype(BF16), preferred_element_type=F32) for kt in kt_refs]
    s_new = lax.dot_general(wq, kn_ref[...].astype(BF16), NT_DIMS, preferred_element_type=F32)
    qry_t = lax.broadcasted_iota(jnp.int32, s_new.shape, 0) % tq
    key_t = lax.broadcasted_iota(jnp.int32, s_new.shape, 1)
    s_new = jnp.where(key_t <= qry_t, s_new, -jnp.inf)
    m_past = s_past[0]
    for s in s_past[1:]:
        m_past = jnp.maximum(m_past, s)
    m = jnp.maximum(jnp.max(m_past, axis=1, keepdims=True), jnp.max(s_new, axis=1, keepdims=True))
    p_past = [jnp.exp(s - m) for s in s_past]
    p_new = jnp.exp(s_new - m)
    p_sum = p_past[0]
    for p in p_past[1:]:
        p_sum = p_sum + p
    den = jnp.sum(p_sum, axis=1, keepdims=True) + jnp.sum(p_new, axis=1, keepdims=True)
    p_past = [p.astype(BF16) for p in p_past]
    p_new = p_new.astype(BF16)
    vn = vn_ref[...].astype(BF16)
    lam = _lambda(lamv_ref, lam_init)
    for h in range(heads):
        rs = slice(h * rows, (h + 1) * rows)
        cs = slice(h * LANES, (h + 1) * LANES)
        acc = jnp.dot(p_new[rs], vn[:, cs], preferred_element_type=F32)
        for p, vr in zip(p_past, v_refs):
            v_h = vr[pl.ds(h, page, stride=heads), :]
            acc = acc + jnp.dot(p[rs], v_h.astype(BF16), preferred_element_type=F32)
        acc = acc / den[rs]
        o = acc[:tq] - lam * acc[tq:]
        o_ref[:, cs] = _rms(o, SUBLN_EPS) * g_ref[...] * (1.0 - lam_init)


def _paged_attention(wq, k_new, v_new, cache_kt, cache_v, page_table, lamv, subln, *, layer, lam_init):
    bsz, tq, dw = k_new.shape
    n_pages = page_table.shape[1]

    def page_spec(shape, j):
        return pl.BlockSpec((None, None) + shape, lambda b, pt: (layer, pt[b * n_pages + j]) + (0,) * len(shape))

    grid_spec = pltpu.PrefetchScalarGridSpec(
        num_scalar_prefetch=1,
        grid=(bsz,),
        in_specs=[
            pl.BlockSpec(lamv.shape, lambda b, pt: (0, 0)),
            pl.BlockSpec((1, LANES), lambda b, pt: (0, 0)),
            pl.BlockSpec((None,) + wq.shape[1:], lambda b, pt: (b, 0, 0)),
            pl.BlockSpec((None, tq, dw), lambda b, pt: (b, 0, 0)),
            pl.BlockSpec((None, tq, dw), lambda b, pt: (b, 0, 0)),
        ] + [page_spec(cache_kt.shape[2:], j) for j in range(n_pages)]
          + [page_spec(cache_v.shape[2:], j) for j in range(n_pages)],
        out_specs=pl.BlockSpec((None, tq, dw), lambda b, pt: (b, 0, 0)),
    )
    return pl.pallas_call(
        functools.partial(_paged_kernel, n_pages=n_pages, lam_init=lam_init),
        grid_spec=grid_spec,
        out_shape=jax.ShapeDtypeStruct((bsz, tq, dw), F32),
        compiler_params=_params("parallel"),
        name="paged_diff_attention",
    )(page_table.reshape(-1), lamv, subln.reshape(1, LANES), wq, k_new, v_new,
      *([cache_kt] * n_pages), *([cache_v] * n_pages))


def _merge_kernel(oa_ref, ob_ref, ga_ref, gb_ref, x_ref, wpa_ref, wpb_ref, wo_ref, g_ref, o_ref):
    m = ga_ref[...] * _dot(oa_ref[...], wpa_ref[...]) + gb_ref[...] * _dot(ob_ref[...], wpb_ref[...])
    a = _dot(m, wo_ref[...])
    o_ref[...] = x_ref[...] + _rms(a, NORM_EPS) * g_ref[...]


def _merge(o_a, o_b, gates, x, lp, *, tm):
    n, d = x.shape
    wd = o_a.shape[1]
    row = lambda width, j=0: pl.BlockSpec((tm, width), lambda i: (i, j))
    full = lambda shape: pl.BlockSpec(shape, lambda i: (0, 0))
    return pl.pallas_call(
        _merge_kernel,
        grid=(n // tm,),
        in_specs=[row(wd), row(wd), row(d, 0), row(d, 1), row(d),
                  full((wd, d)), full((wd, d)), full((d, d)), full((1, d))],
        out_specs=row(d),
        out_shape=jax.ShapeDtypeStruct((n, d), F32),
        compiler_params=_params("parallel"),
        name="merge",
    )(o_a, o_b, gates, gates, x, lp["w_pa"], lp["w_pb"], lp["w_o"], lp["norm_mix_post"].reshape(1, d))


def _ffn_kernel(*refs, seq_len, short, n_blocks):
    if short:
        x_ref, g1_ref, wu_ref, wz_ref, cw_ref, cb_ref, wd_ref, g2_ref, pad_ref, o_ref, u_ref, h_ref, acc_ref = refs
    else:
        (x_ref, g1_ref, wu_ref, wz_ref, cw_ref, cb_ref, wd_ref, g2_ref, cprev_ref,
         o_ref, last_ref, h_ref, acc_ref, carry_ref) = refs
    i = pl.program_id(0)
    c = pl.program_id(1)
    tm = x_ref.shape[0]

    @pl.when(c == 0)
    def _():
        h_ref[...] = (_rms(x_ref[...], NORM_EPS) * g1_ref[...]).astype(BF16)
        acc_ref[...] = jnp.zeros(acc_ref.shape, F32)

    if not short:
        @pl.when(i % (seq_len // tm) == 0)
        def _():
            carry_ref[c, pl.ds(SUBLANES - 2, 2), :] = cprev_ref[...]

    rb = tm // n_blocks
    cw = cw_ref[...]
    us, zs = [None] * n_blocks, [None] * n_blocks

    def up(r):
        h = h_ref[r * rb:(r + 1) * rb, :]
        us[r] = jnp.dot(h, wu_ref[...], preferred_element_type=F32)
        zs[r] = jnp.dot(h, wz_ref[...], preferred_element_type=F32)

    def gate(r):
        u, z = us[r], zs[r]
        rows = slice(r * rb, (r + 1) * rb)
        row = lax.broadcasted_iota(jnp.int32, (rb, 1), 0)
        if short:
            pad = pad_ref[rows, :]
            t = row % seq_len
            s1 = jnp.where(t == 0, pltpu.roll(pad, rb - 1, 0), pltpu.roll(u, 1, 0))
            s2 = jnp.where(t < 2, pad, pltpu.roll(u, 2, 0))
            u_ref[rows, :] = u
        else:
            before = carry_ref[c, pl.ds(SUBLANES - 2, 2), :] if r == 0 else us[r - 1][rb - 2:rb]
            p2 = before[0:1]
            p1 = before[1:2]
            s1 = jnp.where(row == 0, p1, pltpu.roll(u, 1, 0))
            s2 = jnp.where(row == 0, p2, jnp.where(row == 1, p1, pltpu.roll(u, 2, 0)))
        uc = cb_ref[...] + s2 * cw[0:1] + s1 * cw[1:2] + u * cw[2:3]
        act = 0.5 * uc * (1.0 + lax.erf(uc * (2.0 ** -0.5))) * z
        acc_ref[rows, :] += jnp.dot(act.astype(BF16), wd_ref[...], preferred_element_type=F32)

    up(0)
    for r in range(n_blocks):
        if r + 1 < n_blocks:
            up(r + 1)
        gate(r)
    if not short:
        carry_ref[c] = us[-1][rb - SUBLANES:rb]
        last_ref[...] = us[-1][rb - 2:rb]

    @pl.when(c == pl.num_programs(1) - 1)
    def _():
        o_ref[...] = x_ref[...] + _rms(acc_ref[...], NORM_EPS) * g2_ref[...]


def _ffn(x, conv_prev, lp, *, bsz, t, tm, tf):
    n, d = x.shape
    dff = lp["w_down"].shape[0]
    kw = lp["conv_w"].shape[0]
    assert kw == 3 and dff % tf == 0 and n % tm == 0 and t >= 2
    nc = dff // tf
    short = t < tm
    unit = t if short else SUBLANES
    n_blocks = FFN_ROW_BLOCKS if tm % (FFN_ROW_BLOCKS * unit) == 0 else 1
    in_specs = [
        pl.BlockSpec((tm, d), lambda i, c: (i, 0)),
        pl.BlockSpec((1, d), lambda i, c: (0, 0)),
        pl.BlockSpec((d, tf), lambda i, c: (0, c)),
        pl.BlockSpec((d, tf), lambda i, c: (0, nc + c)),
        pl.BlockSpec((kw, tf), lambda i, c: (0, c)),
        pl.BlockSpec((1, tf), lambda i, c: (0, c)),
        pl.BlockSpec((tf, d), lambda i, c: (c, 0)),
        pl.BlockSpec((1, d), lambda i, c: (0, 0)),
    ]
    args = [x, lp["norm_ffn_pre"].reshape(1, d), lp["w_up"], lp["w_up"], lp["conv_w"],
            lp["conv_b"].reshape(1, dff), lp["w_down"], lp["norm_ffn_post"].reshape(1, d)]
    x_spec = pl.BlockSpec((tm, d), lambda i, c: (i, 0))
    scratch = [pltpu.VMEM((tm, d), BF16), pltpu.VMEM((tm, d), F32)]
    if short:
        assert tm % t == 0
        pad = jnp.concatenate([conv_prev, jnp.zeros((bsz, t - 2, dff), F32)], axis=1).reshape(n, dff)
        in_specs.append(pl.BlockSpec((tm, tf), lambda i, c: (i, c)))
        args.append(pad)
        out_specs = [x_spec, pl.BlockSpec((tm, tf), lambda i, c: (i, c))]
        out_shape = [jax.ShapeDtypeStruct((n, d), F32), jax.ShapeDtypeStruct((n, dff), F32)]
    else:
        assert t % tm == 0
        tiles = t // tm
        in_specs.append(pl.BlockSpec((None, 2, tf), lambda i, c: (i // tiles, 0, c)))
        args.append(conv_prev)
        out_specs = [x_spec, pl.BlockSpec((None, 2, tf), lambda i, c: (i, 0, c))]
        out_shape = [jax.ShapeDtypeStruct((n, d), F32), jax.ShapeDtypeStruct((n // tm, 2, dff), F32)]
        scratch.append(pltpu.VMEM((nc, SUBLANES, tf), F32))
    x_new, aux = pl.pallas_call(
        functools.partial(_ffn_kernel, seq_len=t, short=short, n_blocks=n_blocks),
        grid=(n // tm, nc),
        in_specs=in_specs,
        out_specs=out_specs,
        out_shape=out_shape,
        scratch_shapes=scratch,
        compiler_params=_params("arbitrary", "arbitrary"),
        name="ffn",
    )(*args)
    if short:
        conv_new = aux.reshape(bsz, t, dff)[:, t - 2:]
    else:
        conv_new = aux.reshape(bsz, t // tm, 2, dff)[:, -1]
    return x_new, conv_new


def _rope_tables(pos, rows, width):
    inv = jnp.exp(-math.log(ROPE_THETA) * jnp.arange(HALF_ROT, dtype=F32) * (2.0 / ROT_DIM))
    ang = pos.astype(F32)[:, None] * inv[None, :]
    cos, sin = jnp.cos(ang), jnp.sin(ang)
    t = pos.shape[0]
    zeros = jnp.zeros((t, HALF_ROT), F32)
    rest = DA_SUB - ROT_DIM
    cos_t = jnp.concatenate([cos, cos, jnp.ones((t, rest), F32)], axis=1)
    sup_t = jnp.concatenate([-sin, zeros, jnp.zeros((t, rest), F32)], axis=1)
    sdn_t = jnp.concatenate([zeros, sin, jnp.zeros((t, rest), F32)], axis=1)
    return tuple(jnp.tile(tab, (rows // t, width // DA_SUB)) for tab in (cos_t, sup_t, sdn_t))


def _tiles(n, t):
    tm = next(c for c in (1024, 512, 256, 128) if n % c == 0 and (t % c == 0 or c % t == 0))
    return tm, min(tm, 512), 256


def _rwkv_blocking(bsz, t):
    chunk = min(t, 64)
    nb = 2 if chunk == 64 else 8
    cps = next(n for n in (4, 2, 1) if t % (n * chunk) == 0)
    return chunk, math.gcd(nb, bsz), cps


def _group_layer(x, bsz, t, lp, state, rope, consts, attention):
    shift_prev, wkv_prev, conv_prev = state
    tm, tm_merge, tf = _tiles(x.shape[0], t)
    chunk, nb, cps = _rwkv_blocking(bsz, t)
    g = lp["norm_mix_pre"]
    sw = lp["rw_mu"].shape[0]
    rw, gates = _rw_gates(x, g, lp["w_rwg"], sw, tm=tm_merge)
    q, k, v = _qkv(x, g, lp["w_qkv"], rope, tm=tm)
    o_a, wkv_new = _rwkv(rw.reshape(bsz, t, sw), shift_prev, wkv_prev, lp, consts, chunk=chunk, nb=nb, cps=cps)
    dw = v.shape[1]
    o_b = attention(q, k, v)
    x = _merge(o_a.reshape(bsz * t, -1), o_b, gates, x, lp, tm=tm_merge)
    x, conv_new = _ffn(x, conv_prev, lp, bsz=bsz, t=t, tm=tm, tf=tf)
    k_new = k.reshape(bsz, t, dw // (2 * DA_SUB), 2, DA_SUB)
    v_new = v.reshape(bsz, t, dw // (2 * DA_SUB), 2 * DA_SUB)
    shift_new = rw.reshape(bsz, t, sw)[:, t - 1]
    return x, (k_new, v_new, wkv_new, shift_new, conv_new)


def kernel(x_prompt, x_sample, cache_k, cache_v, state_wkv, state_shift, state_conv, page_table, norm_mix_pre, norm_mix_post, norm_ffn_pre, norm_ffn_post, w_in, rw_mu, rw_w0, rw_w_lora, rw_a0, rw_a_lora, rw_g_lora, rw_k_k, rw_k_a, rw_r_k, rw_lnx_w, rw_lnx_b, da_lam_q1, da_lam_k1, da_lam_q2, da_lam_k2, da_subln, w_pa, w_pb, w_o, w_up, conv_w, conv_b, w_down):
    bp, tp, d = x_prompt.shape
    bs, ts, _ = x_sample.shape
    depth = w_in.shape[0]
    rww = rw_w0.shape[1]
    sw = rw_mu.shape[1]
    wl, al = rw_w_lora.shape[1], rw_a_lora.shape[1]
    dw = w_pb.shape[1]
    dff = w_down.shape[1]
    n_pages, page = page_table.shape[1], cache_k.shape[2]
    past = n_pages * page
    assert wl + al == LANES and sw == 3 * rww + wl + al + rw_g_lora.shape[1] and rww % PAIR == 0
    assert dw % LANES == 0 and w_in.shape[2] == sw + 3 * dw + 2 * d

    rope_p = _rope_tables(jnp.arange(tp, dtype=jnp.int32), max(tp, _tiles(bp * tp, tp)[0]), dw)
    rope_s = _rope_tables(past + jnp.arange(ts, dtype=jnp.int32), max(ts, _tiles(bs * ts, ts)[0]), dw)
    consts_p = _rwkv_consts(*_rwkv_blocking(bp, tp), rww)
    consts_s = _rwkv_consts(*_rwkv_blocking(bs, ts), rww)
    cache_kt = jnp.transpose(cache_k, (0, 1, 3, 4, 5, 2)).reshape(depth, cache_k.shape[1], dw, page)
    cache_vr = cache_v.reshape(depth, cache_v.shape[1], page * (dw // LANES), LANES)
    sub_sel = (jnp.arange(dw // DA_SUB)[:, None] == jnp.arange(dw)[None, :] // DA_SUB).astype(F32)

    xp = x_prompt.reshape(bp * tp, d)
    xs = x_sample.reshape(bs * ts, d)
    state_p = (jnp.zeros((bp, sw), F32), jnp.zeros((bp, rww // RW_HEAD, RW_HEAD, RW_HEAD), F32),
               jnp.zeros((bp, 2, dff), F32))
    outs_p, outs_s = [], []
    for l in range(depth):
        wi = w_in[l].astype(BF16)
        lp = {
            "norm_mix_pre": norm_mix_pre[l], "norm_mix_post": norm_mix_post[l],
            "norm_ffn_pre": norm_ffn_pre[l], "norm_ffn_post": norm_ffn_post[l],
            "w_rwg": jnp.concatenate([wi[:, :sw], wi[:, sw + 3 * dw:]], axis=1), "w_qkv": wi[:, sw:sw + 3 * dw],
            "rw_mu": rw_mu[l], "rw_w0": rw_w0[l], "rw_a0": rw_a0[l],
            "wlora_pad": jnp.concatenate([rw_w_lora[l], jnp.zeros((al, rww), F32)], axis=0).astype(BF16),
            "alora_pad": jnp.concatenate([jnp.zeros((wl, rww), F32), rw_a_lora[l]], axis=0).astype(BF16),
            "rw_g_lora": rw_g_lora[l].astype(BF16),
            "rw_k_k": rw_k_k[l], "rw_k_a": rw_k_a[l], "rw_r_k": rw_r_k[l].reshape(-1),
            "rw_lnx_w": rw_lnx_w[l], "rw_lnx_b": rw_lnx_b[l],
            "w_pa": w_pa[l].astype(BF16), "w_pb": w_pb[l].astype(BF16), "w_o": w_o[l].astype(BF16),
            "w_up": w_up[l].astype(BF16), "conv_w": conv_w[l], "conv_b": conv_b[l],
            "w_down": w_down[l].astype(BF16),
        }
        lam_init = 0.8 - 0.6 * math.exp(-0.3 * l)
        lamv = jnp.stack([da_lam_q1[l], da_lam_k1[l], da_lam_q2[l], da_lam_k2[l]])
        subln = da_subln[l]

        def attn_prompt(q, k, v):
            return _flash_attention(q, k, v, lamv, subln, bsz=bp, t=tp, lam_init=lam_init, tile=min(tp, 1024), hb=2)

        def attn_sample(q, k, v):
            wq = (q.reshape(bs, 1, ts, dw) * sub_sel[None, :, None, :]).reshape(bs, -1, dw)
            o = _paged_attention(wq, k.reshape(bs, ts, dw), v.reshape(bs, ts, dw), cache_kt, cache_vr,
                                 page_table, lamv, subln, layer=l, lam_init=lam_init)
            return o.reshape(bs * ts, dw)

        xp, out_p = _group_layer(xp, bp, tp, lp, state_p, rope_p, consts_p, attn_prompt)
        xs, out_s = _group_layer(xs, bs, ts, lp, (state_shift[l], state_wkv[l], state_conv[l]), rope_s, consts_s,
                                 attn_sample)
        outs_p.append(out_p)
        outs_s.append(out_s)

    stack = lambda outs, j: jnp.stack([o[j] for o in outs])
    return ((xp.reshape(bp, tp, d), xs.reshape(bs, ts, d))
            + tuple(stack(outs_p, j) for j in range(5)) + tuple(stack(outs_s, j) for j in range(5)))
```
